```python
import math, functools
import jax, jax.numpy as jnp
from jax import lax
import numpy as np

D_MODEL = 4096
BATCH = 1
SEQ = 8192
DEPTH = 1
DEC_BATCH = 128
DEC_SEQ = 1
PAST_LEN = 2048
PAGE_SIZE = 128

MIX_WIDTH = D_MODEL
ATT_WIDTH = MIX_WIDTH // 2
REC_WIDTH = MIX_WIDTH - ATT_WIDTH
ATT_HEAD_DIM = 128
N_ATT_HEADS = ATT_WIDTH // (2 * ATT_HEAD_DIM)
ATT_SCALE = ATT_HEAD_DIM ** -0.5
REC_EXPAND = 128
N_REC_HEADS = REC_WIDTH // REC_EXPAND
REC_KDIM = REC_EXPAND
REC_VDIM = REC_WIDTH // N_REC_HEADS
D_FF = 4 * D_MODEL
CHUNK = 64
Q_BLOCK = 128
EPS = 1e-6
IN_COLS = 3 * ATT_WIDTH + 4 * REC_WIDTH
SPLITS = (ATT_WIDTH, 2 * ATT_WIDTH, 3 * ATT_WIDTH, 3 * ATT_WIDTH + REC_WIDTH,
          3 * ATT_WIDTH + 2 * REC_WIDTH, 3 * ATT_WIDTH + 3 * REC_WIDTH)
F32 = jnp.float32

kernel_name = "hymba_diffattn_hgrn2_decode_step"


def rms_norm(x, w):
    xf = x.astype(F32)
    y = xf * lax.rsqrt(jnp.mean(xf * xf, axis=-1, keepdims=True) + EPS)
    return (y * w.astype(F32)).astype(x.dtype)


def diff_attn_core(q, k, v, mask, lam):
    s = jnp.einsum('blhcd,bthcd->bhclt', q, k) * ATT_SCALE
    p = jax.nn.softmax(jnp.where(mask, s, -jnp.inf), axis=-1)
    w = p[:, :, 0] - lam * p[:, :, 1]
    return jnp.einsum('bhlt,bthe->blhe', w, v)


def prompt_attention(q, k, v, lam):
    B, L = q.shape[:2]
    nb = L // Q_BLOCK
    qb = jnp.moveaxis(q.reshape(B, nb, Q_BLOCK, N_ATT_HEADS, 2, ATT_HEAD_DIM), 1, 0)
    key_pos = jnp.arange(L)

    def one_block(args):
        qi, i = args
        q_pos = i * Q_BLOCK + jnp.arange(Q_BLOCK)
        return diff_attn_core(qi, k, v, q_pos[:, None] >= key_pos[None, :], lam)

    o = lax.map(one_block, (qb, jnp.arange(nb)))
    return jnp.moveaxis(o, 0, 1).reshape(B, L, N_ATT_HEADS, 2 * ATT_HEAD_DIM)


def sample_attention(q, k, v, lam, pool_k, pool_v, page_table):
    L = q.shape[1]
    past = page_table.shape[1] * pool_k.shape[1]
    mask = jnp.concatenate([jnp.ones((L, past), bool), jnp.tril(jnp.ones((L, L), bool))], axis=1)

    def one_seq(args):
        qi, ki, vi, pages = args
        kp = pool_k[pages].reshape(past, N_ATT_HEADS, 2, ATT_HEAD_DIM).astype(F32)
        vp = pool_v[pages].reshape(past, N_ATT_HEADS, 2 * ATT_HEAD_DIM).astype(F32)
        k_all = jnp.concatenate([kp, ki], axis=0)[None]
        v_all = jnp.concatenate([vp, vi], axis=0)[None]
        return diff_attn_core(qi[None], k_all, v_all, mask, lam)[0]

    return lax.map(one_seq, (q, k, v, page_table))


def hgrn2_chunked(q, log_f, k, v, s0):
    B, L, H, _ = q.shape
    V = v.shape[-1]
    c = min(CHUNK, L)
    n = -(-L // c)
    pad = n * c - L

    def blocks(t):
        t = jnp.pad(t, ((0, 0), (0, pad), (0, 0), (0, 0)))
        return jnp.moveaxis(t.reshape(B, n, c, H, t.shape[-1]), 1, 0)

    causal = jnp.tril(jnp.ones((c, c), bool))[None, :, :, None, None]

    def step(S, blk):
        qc, gc, kc, vc = blk
        b = jnp.cumsum(gc, axis=1)
        o_inter = jnp.einsum('bthk,bhkv->bthv', qc * jnp.exp(b), S)
        decay = jnp.exp(jnp.where(causal, b[:, :, None] - b[:, None, :], -jnp.inf))
        a = jnp.einsum('bthk,btshk->btsh', qc, decay * kc[:, None])
        o_intra = jnp.einsum('btsh,bshv->bthv', a, vc)
        b_last = b[:, -1]
        S = jnp.exp(b_last)[..., None] * S + jnp.einsum('bshk,bshv->bhkv', kc * jnp.exp(b_last[:, None] - b), vc)
        return S, o_inter + o_intra

    S, o = lax.scan(step, s0.astype(F32), (blocks(q), blocks(log_f), blocks(k), blocks(v)))
    o = jnp.moveaxis(o, 0, 1).reshape(B, n * c, H, V)[:, :L]
    return o, S


def hybrid_layer(x, s0, attend, layer, norm1_w, w_in, lq1, lk1, lq2, lk2, subln_w, lb,
                 rec_norm_w, w_out, norm2_w, w_up, w_down):
    B, L, _ = x.shape
    xn = rms_norm(x, norm1_w)
    z = xn @ w_in
    aq, ak, av, rq, rf, ri, rg = jnp.split(z, SPLITS, axis=-1)

    lam_init = 0.8 - 0.6 * math.exp(-0.3 * layer)
    lam = (jnp.exp(jnp.sum(lq1.astype(F32) * lk1.astype(F32)))
           - jnp.exp(jnp.sum(lq2.astype(F32) * lk2.astype(F32))) + lam_init)
    q = aq.reshape(B, L, N_ATT_HEADS, 2, ATT_HEAD_DIM).astype(F32)
    k_store = ak.reshape(B, L, N_ATT_HEADS, 2 * ATT_HEAD_DIM)
    v_store = av.reshape(B, L, N_ATT_HEADS, 2 * ATT_HEAD_DIM)
    o_att = attend(q, k_store.reshape(B, L, N_ATT_HEADS, 2, ATT_HEAD_DIM).astype(F32),
                   v_store.astype(F32), lam)
    o_att = rms_norm(o_att, subln_w) * (1.0 - lam_init)

    lb_h = lb.reshape(N_REC_HEADS, REC_KDIM)
    f = lb_h + (1.0 - lb_h) * jax.nn.sigmoid(rf.reshape(B, L, N_REC_HEADS, REC_KDIM).astype(F32))
    qr = jax.nn.silu(rq.reshape(B, L, N_REC_HEADS, REC_KDIM).astype(F32))
    vr = ri.reshape(B, L, N_REC_HEADS, REC_VDIM).astype(F32)
    o_rec, s_new = hgrn2_chunked(qr, jnp.log(f), 1.0 - f, vr, s0)
    o_rec = rms_norm(o_rec, rec_norm_w) * jax.nn.silu(rg.reshape(B, L, N_REC_HEADS, REC_VDIM).astype(F32))

    mix = jnp.concatenate([o_att.reshape(B, L, ATT_WIDTH), o_rec.reshape(B, L, REC_WIDTH)], axis=-1)
    h = x + mix.astype(x.dtype) @ w_out
    hn = rms_norm(h, norm2_w)
    y = h + jnp.square(jax.nn.relu(hn @ w_up)) @ w_down
    return y, k_store, v_store, s_new


def setup_inputs(seed: int = 0) -> dict:
    key = jax.random.key(seed)
    ks = jax.random.split(key, 20)
    n_pages = PAST_LEN // PAGE_SIZE
    n_used = DEC_BATCH * n_pages
    n_phys = n_used + max(1, n_used // 4)

    def nrm(k, shape, scale):
        return scale * jax.random.normal(k, shape, F32)

    def gain(k, shape):
        return 1.0 + 0.02 * jax.random.normal(k, shape, F32)

    page_table = jax.random.permutation(ks[5], n_phys)[:n_used].reshape(DEC_BATCH, n_pages).astype(jnp.int32)
    return {
        "x_prompt": nrm(ks[0], (BATCH, SEQ, D_MODEL), 1.0),
        "x_sample": nrm(ks[1], (DEC_BATCH, DEC_SEQ, D_MODEL), 1.0),
        "cache_k": nrm(ks[2], (DEPTH, n_phys, PAGE_SIZE, N_ATT_HEADS, 2 * ATT_HEAD_DIM), 1.0),
        "cache_v": nrm(ks[3], (DEPTH, n_phys, PAGE_SIZE, N_ATT_HEADS, 2 * ATT_HEAD_DIM), 1.0),
        "state_hgrn": nrm(ks[4], (DEPTH, DEC_BATCH, N_REC_HEADS, REC_KDIM, REC_VDIM), 0.5),
        "page_table": page_table,
        "norm1_w": gain(ks[6], (DEPTH, D_MODEL)),
        "w_in": nrm(ks[7], (DEPTH, D_MODEL, IN_COLS), D_MODEL ** -0.5),
        "lambda_q1": nrm(ks[8], (DEPTH, ATT_HEAD_DIM), 0.1),
        "lambda_k1": nrm(ks[9], (DEPTH, ATT_HEAD_DIM), 0.1),
        "lambda_q2": nrm(ks[10], (DEPTH, ATT_HEAD_DIM), 0.1),
        "lambda_k2": nrm(ks[11], (DEPTH, ATT_HEAD_DIM), 0.1),
        "subln_w": gain(ks[12], (DEPTH, 2 * ATT_HEAD_DIM)),
        "lb_logits": nrm(ks[13], (DEPTH + 1, REC_WIDTH), 1.0),
        "rec_norm_w": gain(ks[14], (DEPTH, REC_VDIM)),
        "w_out": nrm(ks[15], (DEPTH, MIX_WIDTH, D_MODEL), MIX_WIDTH ** -0.5),
        "norm2_w": gain(ks[16], (DEPTH, D_MODEL)),
        "w_up": nrm(ks[17], (DEPTH, D_MODEL, D_FF), D_MODEL ** -0.5),
        "w_down": nrm(ks[18], (DEPTH, D_FF, D_MODEL), D_FF ** -0.5),
        "final_norm_w": gain(ks[19], (D_MODEL,)),
    }


def reference(x_prompt, x_sample, cache_k, cache_v, state_hgrn, page_table, norm1_w, w_in,
              lambda_q1, lambda_k1, lambda_q2, lambda_k2, subln_w, lb_logits, rec_norm_w,
              w_out, norm2_w, w_up, w_down, final_norm_w):
    lb_all = jnp.cumsum(jax.nn.softmax(lb_logits.astype(F32), axis=0), axis=0)
    hp, hs = x_prompt, x_sample
    kp_l, vp_l, sp_l, ks_l, vs_l, ss_l = [], [], [], [], [], []
    for l in range(DEPTH):
        params = (norm1_w[l], w_in[l], lambda_q1[l], lambda_k1[l], lambda_q2[l], lambda_k2[l],
                  subln_w[l], lb_all[l], rec_norm_w[l], w_out[l], norm2_w[l], w_up[l], w_down[l])
        s0p = jnp.zeros((hp.shape[0], N_REC_HEADS, REC_KDIM, REC_VDIM), F32)
        hp, kp, vp, sp = hybrid_layer(hp, s0p, prompt_attention, l, *params)
        attend_s = functools.partial(sample_attention, pool_k=cache_k[l], pool_v=cache_v[l],
                                     page_table=page_table)
        hs, ksn, vsn, ssn = hybrid_layer(hs, state_hgrn[l], attend_s, l, *params)
        kp_l.append(kp); vp_l.append(vp); sp_l.append(sp)
        ks_l.append(ksn); vs_l.append(vsn); ss_l.append(ssn)
    y_prompt = rms_norm(hp, final_norm_w)
    y_sample = rms_norm(hs, final_norm_w)
    return (y_prompt, y_sample, jnp.stack(kp_l), jnp.stack(vp_l), jnp.stack(sp_l),
            jnp.stack(ks_l), jnp.stack(vs_l), jnp.stack(ss_l))
```

```python
import functools
import math

import jax
import jax.numpy as jnp
from jax import lax
from jax.experimental import pallas as pl
from jax.experimental.pallas import tpu as pltpu

F32 = jnp.float32
BF16 = jnp.bfloat16
EPS = 1e-6
LANES = 128
SUBLANES = 8
VMEM_CAP_BYTES = 60000 * 1024
NEG_INF = float("-inf")


def _cparams(semantics, vmem_bytes):
    return pltpu.CompilerParams(dimension_semantics=semantics,
                                vmem_limit_bytes=int(min(VMEM_CAP_BYTES, vmem_bytes)))


def _dot(a, b):
    return jnp.dot(a, b, preferred_element_type=F32)


def _dot_nt(a, b):
    return lax.dot_general(a, b, (((1,), (1,)), ((), ())), preferred_element_type=F32)


def _dot_tn(a, b):
    return lax.dot_general(a, b, (((0,), (0,)), ((), ())), preferred_element_type=F32)


def _sigmoid(x):
    return 1.0 / (1.0 + jnp.exp(-x))


def _silu(x):
    return x * _sigmoid(x)


def _rmsnorm_kernel(x_ref, w_ref, o_ref):
    x = x_ref[...]
    ms = jnp.mean(x * x, axis=-1, keepdims=True)
    o_ref[...] = (x * lax.rsqrt(ms + EPS) * w_ref[...]).astype(o_ref.dtype)


def _rmsnorm(x, w, *, bm):
    m, d = x.shape
    return pl.pallas_call(
        _rmsnorm_kernel,
        grid=(m // bm,),
        in_specs=[pl.BlockSpec((bm, d), lambda i: (i, 0)),
                  pl.BlockSpec((1, d), lambda i: (0, 0))],
        out_specs=pl.BlockSpec((bm, d), lambda i: (i, 0)),
        out_shape=jax.ShapeDtypeStruct((m, d), BF16),
        compiler_params=_cparams(("parallel",), 2 * bm * d * 6 + 2 * bm * d * 4 + (2 << 20)),
        name="rmsnorm",
    )(x, w.reshape(1, d))


def _proj_att_kernel(x_ref, wq_ref, wk_ref, wv_ref, q_ref, k_ref, kb_ref, v_ref, vb_ref, *, scale):
    x = x_ref[...]
    q_ref[...] = (_dot(x, wq_ref[...]) * scale).astype(q_ref.dtype)
    k = _dot(x, wk_ref[...])
    k_ref[...] = k
    kb_ref[...] = k.astype(kb_ref.dtype)
    v = _dot(x, wv_ref[...])
    v_ref[...] = v
    vb_ref[...] = v.astype(vb_ref.dtype)


def _proj_att(xn, w_in, *, att_width, scale, bm, bn):
    m, d = xn.shape
    nb = att_width // bn
    xspec = pl.BlockSpec((bm, d), lambda i, j: (i, 0))

    def wspec(g):
        return pl.BlockSpec((d, bn), lambda i, j: (0, g * nb + j))

    ospec = pl.BlockSpec((bm, bn), lambda i, j: (i, j))
    f32o = jax.ShapeDtypeStruct((m, att_width), F32)
    b16o = jax.ShapeDtypeStruct((m, att_width), BF16)
    vmem = 2 * (bm * d * 2 + 3 * d * bn * 2 + bm * bn * 14) + 6 * bm * bn * 4 + (2 << 20)
    return pl.pallas_call(
        functools.partial(_proj_att_kernel, scale=scale),
        grid=(m // bm, nb),
        in_specs=[xspec, wspec(0), wspec(1), wspec(2)],
        out_specs=[ospec] * 5,
        out_shape=[b16o, f32o, b16o, f32o, b16o],
        compiler_params=_cparams(("parallel", "arbitrary"), vmem),
        name="proj_att",
    )(xn, w_in, w_in, w_in)


def _proj_rec_kernel(x_ref, wq_ref, wf_ref, wi_ref, wg_ref, lb_ref, q_ref, g_ref, k_ref, v_ref, t_ref, *, hd,
                     layer):
    x = x_ref[...]
    lbl = lb_ref[...]
    e = jnp.exp(lbl - jnp.max(lbl, axis=0, keepdims=True))
    lb = jnp.sum(e[:layer + 1, :], axis=0, keepdims=True) / jnp.sum(e, axis=0, keepdims=True)
    f = lb + (1.0 - lb) * _sigmoid(_dot(x, wf_ref[...]))
    qr = _silu(_dot(x, wq_ref[...]))
    vr = _dot(x, wi_ref[...])
    gate = _silu(_dot(x, wg_ref[...]))
    logf = jnp.log(f)
    kk = 1.0 - f
    for h in range(q_ref.shape[0]):
        sl = slice(h * hd, (h + 1) * hd)
        q_ref[h] = qr[:, sl]
        g_ref[h] = logf[:, sl]
        k_ref[h] = kk[:, sl]
        v_ref[h] = vr[:, sl]
        t_ref[h] = gate[:, sl]


def _proj_rec(xn, w_in, lb_logits, *, layer, col0, rec_width, hd, bm, bn):
    m, d = xn.shape
    nb = rec_width // bn
    c0 = col0 // bn
    hb = bn // hd
    xspec = pl.BlockSpec((bm, d), lambda i, j: (i, 0))

    def wspec(g):
        return pl.BlockSpec((d, bn), lambda i, j: (0, c0 + g * nb + j))

    lbspec = pl.BlockSpec((lb_logits.shape[0], bn), lambda i, j: (0, j))
    ospec = pl.BlockSpec((hb, bm, hd), lambda i, j: (j, i, 0))
    oshape = jax.ShapeDtypeStruct((rec_width // hd, m, hd), F32)
    vmem = 2 * (bm * d * 2 + 4 * d * bn * 2 + 5 * bm * bn * 4) + 10 * bm * bn * 4 + (2 << 20)
    return pl.pallas_call(
        functools.partial(_proj_rec_kernel, hd=hd, layer=layer),
        grid=(m // bm, nb),
        in_specs=[xspec, wspec(0), wspec(1), wspec(2), wspec(3), lbspec],
        out_specs=[ospec] * 5,
        out_shape=[oshape] * 5,
        compiler_params=_cparams(("parallel", "arbitrary"), vmem),
        name="proj_rec",
    )(xn, w_in, w_in, w_in, w_in, lb_logits)


def _out_proj_kernel(x_ref, a_ref, r_ref, wa_ref, wr_ref, h_ref):
    h_ref[...] = x_ref[...] + _dot(a_ref[...], wa_ref[...]) + _dot(r_ref[...], wr_ref[...])


def _out_proj(x, o_att, o_rec, w_out, *, bm, bn):
    m, d = x.shape
    ka = o_att.shape[1]
    kr = o_rec.shape[1]
    assert ka == kr
    vmem = 2 * (bm * bn * 8 + bm * (ka + kr) * 2 + (ka + kr) * bn * 2) + 2 * bm * bn * 4 + (2 << 20)
    return pl.pallas_call(
        _out_proj_kernel,
        grid=(m // bm, d // bn),
        in_specs=[pl.BlockSpec((bm, bn), lambda i, j: (i, j)),
                  pl.BlockSpec((bm, ka), lambda i, j: (i, 0)),
                  pl.BlockSpec((bm, kr), lambda i, j: (i, 0)),
                  pl.BlockSpec((ka, bn), lambda i, j: (0, j)),
                  pl.BlockSpec((kr, bn), lambda i, j: (1, j))],
        out_specs=pl.BlockSpec((bm, bn), lambda i, j: (i, j)),
        out_shape=jax.ShapeDtypeStruct((m, d), F32),
        compiler_params=_cparams(("parallel", "arbitrary"), vmem),
        name="out_proj",
    )(x, o_att, o_rec, w_out, w_out)


def _ffn_kernel(h_ref, n2_ref, wu_ref, wd_ref, nf_ref, y_ref, hn_ref, *, final_norm):
    f = pl.program_id(1)

    @pl.when(f == 0)
    def _():
        h = h_ref[...]
        ms = jnp.mean(h * h, axis=-1, keepdims=True)
        hn_ref[...] = (h * lax.rsqrt(ms + EPS) * n2_ref[...]).astype(hn_ref.dtype)
        y_ref[...] = h

    u = jnp.maximum(_dot(hn_ref[...], wu_ref[...]), 0.0)
    y_ref[...] += _dot((u * u).astype(BF16), wd_ref[...])

    if final_norm:
        @pl.when(f == pl.num_programs(1) - 1)
        def _():
            y = y_ref[...]
            ms = jnp.mean(y * y, axis=-1, keepdims=True)
            y_ref[...] = y * lax.rsqrt(ms + EPS) * nf_ref[...]


def _ffn(h, norm2_w, w_up, w_down, final_w, *, final_norm, bm, bf):
    m, d = h.shape
    dff = w_up.shape[1]
    vmem = bm * d * 4 + 2 * bm * d * 4 + bm * d * 2 + 2 * 2 * d * bf * 2 + bm * bf * 8 + bm * d * 4 + (2 << 20)
    return pl.pallas_call(
        functools.partial(_ffn_kernel, final_norm=final_norm),
        grid=(m // bm, dff // bf),
        in_specs=[pl.BlockSpec((bm, d), lambda i, f: (i, 0), pipeline_mode=pl.Buffered(1)),
                  pl.BlockSpec((1, d), lambda i, f: (0, 0)),
                  pl.BlockSpec((d, bf), lambda i, f: (0, f)),
                  pl.BlockSpec((bf, d), lambda i, f: (f, 0)),
                  pl.BlockSpec((1, d), lambda i, f: (0, 0))],
        out_specs=pl.BlockSpec((bm, d), lambda i, f: (i, 0)),
        out_shape=jax.ShapeDtypeStruct((m, d), F32),
        scratch_shapes=[pltpu.VMEM((bm, d), BF16)],
        compiler_params=_cparams(("parallel", "arbitrary"), vmem),
        name="ffn",
    )(h, norm2_w.reshape(1, d), w_up, w_down, final_w.reshape(1, d))


def _lambda_full(lam_ref, lam_init):
    lam = lam_ref[...]
    s1 = jnp.sum(lam[0:1, :] * lam[1:2, :], axis=-1, keepdims=True)
    s2 = jnp.sum(lam[2:3, :] * lam[3:4, :], axis=-1, keepdims=True)
    return jnp.exp(s1) - jnp.exp(s2) + lam_init


def _softmax_step(s, v, m_prev, l_prev, acc_prev):
    m_new = jnp.maximum(m_prev, jnp.max(s, axis=-1, keepdims=True))
    alpha = jnp.exp(m_prev - m_new)
    p = jnp.exp(s - m_new)
    l_new = alpha * l_prev + jnp.sum(p, axis=-1, keepdims=True)
    acc_new = alpha * acc_prev + _dot(p.astype(BF16), v)
    return m_new, l_new, acc_new


def _diff_combine(acc0, l0, acc1, l1, lam, sub_w, out_scale):
    o = acc0 / l0 - lam * (acc1 / l1)
    ms = jnp.mean(o * o, axis=-1, keepdims=True)
    return o * lax.rsqrt(ms + EPS) * sub_w * out_scale


def _prompt_attn_kernel(lam_ref, sub_ref, q_ref, k_ref, v_ref, o_ref, m_ref, l_ref, acc_ref, *,
                        bq, dh, lam_init):
    qi = pl.program_id(1)
    q = q_ref[...]
    qs = (q[:, :dh], q[:, dh:])
    m_ref[...] = jnp.full(m_ref.shape, NEG_INF, F32)
    l_ref[...] = jnp.zeros(l_ref.shape, F32)
    acc_ref[...] = jnp.zeros(acc_ref.shape, F32)

    def block(j, masked):
        start = pl.multiple_of(j * bq, bq)
        kb = k_ref[pl.ds(start, bq), :]
        vb = v_ref[pl.ds(start, bq), :]
        for c in range(2):
            s = _dot_nt(qs[c], kb[:, c * dh:(c + 1) * dh])
            if masked:
                row = lax.broadcasted_iota(jnp.int32, s.shape, 0)
                col = lax.broadcasted_iota(jnp.int32, s.shape, 1)
                s = jnp.where(row >= col, s, NEG_INF)
            m_ref[c], l_ref[c], acc_ref[c] = _softmax_step(s, vb, m_ref[c], l_ref[c], acc_ref[c])

    def body(j, carry):
        block(j, False)
        return carry

    lax.fori_loop(0, qi, body, 0)
    block(qi, True)

    lam = _lambda_full(lam_ref, lam_init)
    o = _diff_combine(acc_ref[0], l_ref[0], acc_ref[1], l_ref[1], lam, sub_ref[...], 1.0 - lam_init)
    o_ref[...] = o.astype(o_ref.dtype)


def _prompt_attn(q, kb, vb, lam_rows, subln_w, *, n_heads, dh, lam_init, bq):
    m, width = q.shape
    hw = 2 * dh
    assert width == n_heads * hw
    vmem = 2 * (2 * m * hw * 2 + 2 * bq * hw * 2) + 2 * bq * hw * 4 + 8 * bq * bq * 4 + (4 << 20)
    return pl.pallas_call(
        functools.partial(_prompt_attn_kernel, bq=bq, dh=dh, lam_init=lam_init),
        grid=(n_heads, m // bq),
        in_specs=[pl.BlockSpec((4, dh), lambda h, i: (0, 0)),
                  pl.BlockSpec((1, hw), lambda h, i: (0, 0)),
                  pl.BlockSpec((bq, hw), lambda h, i: (i, h)),
                  pl.BlockSpec((m, hw), lambda h, i: (0, h)),
                  pl.BlockSpec((m, hw), lambda h, i: (0, h))],
        out_specs=pl.BlockSpec((bq, hw), lambda h, i: (i, h)),
        out_shape=jax.ShapeDtypeStruct((m, width), BF16),
        scratch_shapes=[pltpu.VMEM((2, bq, 1), F32), pltpu.VMEM((2, bq, 1), F32),
                        pltpu.VMEM((2, bq, hw), F32)],
        compiler_params=_cparams(("parallel", "arbitrary"), vmem),
        name="prompt_attn",
    )(lam_rows, subln_w.reshape(1, hw), q, kb, vb)


def _split3_bf16(x):
    hi = x.astype(BF16)
    r1 = x - hi.astype(F32)
    mid = r1.astype(BF16)
    lo = (r1 - mid.astype(F32)).astype(BF16)
    return hi, mid, lo


def _group_row(x, group, row):
    n, lanes = x.shape
    x3 = x.reshape(n // group, group, lanes)
    return jnp.broadcast_to(x3[:, row:row + 1, :], x3.shape).reshape(n, lanes)


def _hgrn_chunk(qr, g, kk, v, st, tril, dsum):
    c, kd = g.shape
    vb = v.astype(BF16)
    g3 = jnp.concatenate(_split3_bf16(g), axis=1)
    b3 = _dot(tril, g3)
    b = b3[:, :kd] + b3[:, kd:2 * kd] + b3[:, 2 * kd:]
    b_last = b[c - 1:c, :]
    row = lax.broadcasted_iota(jnp.int32, (c, kd), 0)
    trow = lax.broadcasted_iota(jnp.int32, (c, c), 0)
    tcol = lax.broadcasted_iota(jnp.int32, (c, c), 1)

    o = _dot_nt((qr * jnp.exp(b)).astype(BF16), st.astype(BF16))
    k_state = (kk * jnp.exp(b_last - b)).astype(BF16)
    st_new = jnp.exp(b_last) * st + _dot_tn(vb, k_state)

    rmod = row % SUBLANES
    parts = []
    for j in range(SUBLANES):
        bj = _group_row(b, SUBLANES, j)
        kj = _group_row(kk, SUBLANES, j)
        pj = qr * kj * jnp.exp(jnp.minimum(b - bj, 0.0))
        parts.append(jnp.where(rmod >= j, pj, 0.0).astype(BF16))
    a = jnp.where(trow // SUBLANES == tcol // SUBLANES, _dot(jnp.concatenate(parts, axis=1), dsum), 0.0)

    m = SUBLANES
    while m < c:
        e = jnp.exp(-jnp.abs(b - _group_row(b, 2 * m, m - 1)))
        right = (row % (2 * m)) >= m
        qm = jnp.where(right, qr * e, 0.0).astype(BF16)
        km = jnp.where(right, 0.0, kk * e).astype(BF16)
        a = a + jnp.where(trow // (2 * m) == tcol // (2 * m), _dot_nt(qm, km), 0.0)
        m *= 2

    o = o + _dot(a.astype(BF16), vb)
    return o, st_new


def _prompt_hgrn_kernel(q_ref, g_ref, k_ref, v_ref, t_ref, nw_ref, tril_ref, dsum_ref, o_ref, s_ref, st_ref, *,
                        chunk):
    ci = pl.program_id(1)

    @pl.when(ci == 0)
    def _():
        st_ref[...] = jnp.zeros(st_ref.shape, F32)

    tril = tril_ref[...]
    dsum = dsum_ref[...]
    st = st_ref[...]
    for sub in range(q_ref.shape[0] // chunk):
        rows = slice(sub * chunk, (sub + 1) * chunk)
        o, st = _hgrn_chunk(q_ref[rows, :], g_ref[rows, :], k_ref[rows, :], v_ref[rows, :], st, tril, dsum)
        ms = jnp.mean(o * o, axis=-1, keepdims=True)
        o_ref[rows, :] = (o * lax.rsqrt(ms + EPS) * nw_ref[...] * t_ref[rows, :]).astype(o_ref.dtype)
    st_ref[...] = st

    @pl.when(ci == pl.num_programs(1) - 1)
    def _():
        s_ref[...] = st.T


def _prompt_hgrn(qr, g, kk, vr, gate, rec_norm_w, *, chunk, bt):
    nh, m, kd = qr.shape
    assert vr.shape[2] == kd
    tril = jnp.tril(jnp.ones((chunk, chunk), F32)).astype(BF16)
    dsum = (jnp.arange(SUBLANES * kd)[:, None] // kd == jnp.arange(chunk)[None, :] % SUBLANES).astype(BF16)
    ispec = pl.BlockSpec((None, bt, kd), lambda h, c: (h, c, 0))
    vmem = 2 * (5 * bt * kd * 4 + bt * kd * 2) + 64 * chunk * kd * 4 + (4 << 20)
    return pl.pallas_call(
        functools.partial(_prompt_hgrn_kernel, chunk=chunk),
        grid=(nh, m // bt),
        in_specs=[ispec] * 5 + [pl.BlockSpec((1, kd), lambda h, c: (0, 0)),
                                pl.BlockSpec((chunk, chunk), lambda h, c: (0, 0)),
                                pl.BlockSpec((SUBLANES * kd, chunk), lambda h, c: (0, 0))],
        out_specs=[pl.BlockSpec((bt, kd), lambda h, c: (c, h)),
                   pl.BlockSpec((None, kd, kd), lambda h, c: (h, 0, 0))],
        out_shape=[jax.ShapeDtypeStruct((m, nh * kd), BF16),
                   jax.ShapeDtypeStruct((nh, kd, kd), F32)],
        scratch_shapes=[pltpu.VMEM((kd, kd), F32)],
        compiler_params=_cparams(("parallel", "arbitrary"), vmem),
        name="prompt_hgrn",
    )(qr, g, kk, vr, gate, rec_norm_w.reshape(1, kd), tril, dsum)


def _swap_halves(x, dh):
    return jnp.concatenate([x[..., dh:], x[..., :dh]], axis=-1)


def _sample_attn_kernel(pt_ref, lam_ref, sub_ref, ones_ref, q_ref, ks_ref, vs_ref, *rest, n_pages_step, dh,
                        lam_init):
    del pt_ref
    k_refs = rest[:n_pages_step]
    v_refs = rest[n_pages_step:2 * n_pages_step]
    o_ref, m_ref, l_ref, same_ref, cross_ref = rest[2 * n_pages_step:]
    step = pl.program_id(1)
    q = q_ref[...].astype(F32)

    @pl.when(step == 0)
    def _():
        m_ref[...] = jnp.full(m_ref.shape, NEG_INF, F32)
        l_ref[...] = jnp.zeros(l_ref.shape, F32)
        same_ref[...] = jnp.zeros(same_ref.shape, F32)
        cross_ref[...] = jnp.zeros(cross_ref.shape, F32)

    def half_sums(x):
        lead = x.shape[:-1]
        flat = x.reshape(-1, x.shape[-1]).astype(BF16)
        return _dot(flat, ones_ref[...]).reshape(*lead, x.shape[-1])

    def absorb(s, v):
        m_prev = m_ref[...]
        m_new = jnp.maximum(m_prev, jnp.max(s, axis=0))
        alpha = jnp.exp(m_prev - m_new)
        p = jnp.exp(s - m_new)
        m_ref[...] = m_new
        l_ref[...] = alpha * l_ref[...] + jnp.sum(p, axis=0)
        same_ref[...] = alpha * same_ref[...] + jnp.sum(p * v, axis=0)
        cross_ref[...] = alpha * cross_ref[...] + jnp.sum(p * _swap_halves(v, dh), axis=0)

    for g in range(n_pages_step):
        absorb(half_sums(k_refs[g][...] * q), v_refs[g][...])

    @pl.when(step == pl.num_programs(1) - 1)
    def _():
        absorb(half_sums((ks_ref[...] * q)[None]), vs_ref[...][None])
        same = same_ref[...]
        cross = cross_ref[...]
        l = l_ref[...]
        acc0 = jnp.concatenate([same[:, :dh], cross[:, :dh]], axis=-1)
        acc1 = jnp.concatenate([cross[:, dh:], same[:, dh:]], axis=-1)
        l0 = jnp.concatenate([l[:, :dh], l[:, :dh]], axis=-1)
        l1 = jnp.concatenate([l[:, dh:], l[:, dh:]], axis=-1)
        lam = _lambda_full(lam_ref, lam_init)
        o = _diff_combine(acc0, l0, acc1, l1, lam, sub_ref[...], 1.0 - lam_init)
        o_ref[...] = o.astype(o_ref.dtype)


def _sample_attn(q, k_self, v_self, pool_k, pool_v, page_ids, lam_rows, subln_w, *, n_heads, dh, lam_init,
                 pages_per_step):
    nb, width = q.shape
    n_pages = page_ids.shape[1]
    page = pool_k.shape[1]
    hw = 2 * dh
    g = pages_per_step
    assert n_pages % g == 0 and pool_k.shape[2:] == (n_heads, hw) and width == n_heads * hw
    ones = (jnp.arange(hw)[:, None] // dh == jnp.arange(hw)[None, :] // dh).astype(BF16)

    def page_spec(i):
        return pl.BlockSpec((None, page, n_heads, hw), lambda b, p, pt: (pt[b, p * g + i], 0, 0, 0))

    row_spec = pl.BlockSpec((None, n_heads, hw), lambda b, p, pt: (b, 0, 0))
    in_specs = ([pl.BlockSpec((4, dh), lambda b, p, pt: (0, 0)),
                 pl.BlockSpec((1, hw), lambda b, p, pt: (0, 0)),
                 pl.BlockSpec((hw, hw), lambda b, p, pt: (0, 0)),
                 row_spec, row_spec, row_spec]
                + [page_spec(i) for i in range(g)] * 2)
    page_bytes = page * width * 4
    vmem = 2 * 2 * g * page_bytes + 8 * page_bytes + (4 << 20)
    scratch = [pltpu.VMEM((n_heads, hw), F32)] * 4
    out = pl.pallas_call(
        functools.partial(_sample_attn_kernel, n_pages_step=g, dh=dh, lam_init=lam_init),
        grid_spec=pltpu.PrefetchScalarGridSpec(
            num_scalar_prefetch=1,
            grid=(nb, n_pages // g),
            in_specs=in_specs,
            out_specs=row_spec,
            scratch_shapes=scratch),
        out_shape=jax.ShapeDtypeStruct((nb, n_heads, hw), BF16),
        compiler_params=_cparams(("parallel", "arbitrary"), vmem),
        name="sample_attn",
    )(page_ids, lam_rows, subln_w.reshape(1, hw), ones, q.reshape(nb, n_heads, hw),
      k_self.reshape(nb, n_heads, hw), v_self.reshape(nb, n_heads, hw), *([pool_k] * g), *([pool_v] * g))
    return out.reshape(nb, width)


def _sample_hgrn_kernel(q_ref, g_ref, k_ref, v_ref, t_ref, nw_ref, s_ref, o_ref, so_ref, orow_ref):
    nh, nb, kd = q_ref.shape
    f_t = jnp.exp(g_ref[...]).reshape(nh * nb, kd).T
    k_t = k_ref[...].reshape(nh * nb, kd).T
    q_t = q_ref[...].reshape(nh * nb, kd).T
    for h in range(nh):
        for i in range(nb):
            r = h * nb + i
            s_new = f_t[:, r:r + 1] * s_ref[i, h] + k_t[:, r:r + 1] * v_ref[h, i:i + 1, :]
            so_ref[i, h] = s_new
            o = jnp.sum(q_t[:, r:r + 1] * s_new, axis=0, keepdims=True)
            ms = jnp.mean(o * o, axis=-1, keepdims=True)
            orow_ref[i:i + 1, h * kd:(h + 1) * kd] = o * lax.rsqrt(ms + EPS) * nw_ref[...] * t_ref[h, i:i + 1, :]
    o_ref[...] = orow_ref[...].astype(o_ref.dtype)


def _sample_hgrn(qr, g, kk, vr, gate, rec_norm_w, state, *, bb):
    nh, nb, kd = qr.shape
    vd = vr.shape[2]
    assert state.shape == (nb, nh, kd, vd) and kd == vd
    ispec = pl.BlockSpec((nh, bb, kd), lambda b: (0, b, 0))
    sspec = pl.BlockSpec((bb, nh, kd, vd), lambda b: (b, 0, 0, 0))
    vmem = 2 * 2 * bb * nh * kd * vd * 4 + 2 * 5 * nh * bb * kd * 4 + 8 * nh * bb * kd * 4 + (4 << 20)
    return pl.pallas_call(
        _sample_hgrn_kernel,
        grid=(nb // bb,),
        in_specs=[ispec] * 5 + [pl.BlockSpec((1, vd), lambda b: (0, 0)), sspec],
        out_specs=[pl.BlockSpec((bb, nh * vd), lambda b: (b, 0)), sspec],
        out_shape=[jax.ShapeDtypeStruct((nb, nh * vd), BF16),
                   jax.ShapeDtypeStruct(state.shape, F32)],
        scratch_shapes=[pltpu.VMEM((bb, nh * vd), F32)],
        compiler_params=_cparams(("parallel",), vmem),
        name="sample_hgrn",
    )(qr, g, kk, vr, gate, rec_norm_w.reshape(1, vd), state)


PROMPT_ROWS = 1024
PROMPT_COLS = 256
PROMPT_Q_BLOCK = 512
HGRN_CHUNK = 128
HGRN_STEP = 512
FFN_ROWS = 512
FFN_COLS = 512
OUT_COLS = 1024
SAMPLE_PAGES_PER_STEP = 4
SAMPLE_HGRN_SEQS = 8


def _mixer_inputs(x, l, dims, norm1_w, w_in_b, lb_logits, *, bm, bn):
    xn = _rmsnorm(x, norm1_w[l], bm=min(bm, 256))
    att = _proj_att(xn, w_in_b[l], att_width=dims["att_width"], scale=dims["dh"] ** -0.5, bm=bm, bn=bn)
    rec = _proj_rec(xn, w_in_b[l], lb_logits, layer=l, col0=3 * dims["att_width"], rec_width=dims["rec_width"],
                    hd=dims["kd"], bm=bm, bn=bn)
    return att, rec


def kernel(x_prompt, x_sample, cache_k, cache_v, state_hgrn, page_table, norm1_w, w_in, lambda_q1, lambda_k1,
           lambda_q2, lambda_k2, subln_w, lb_logits, rec_norm_w, w_out, norm2_w, w_up, w_down, final_norm_w):
    batch, seq, d_model = x_prompt.shape
    dec_batch, dec_seq, _ = x_sample.shape
    depth, n_phys, page, n_att_heads, hw = cache_k.shape
    _, _, n_rec_heads, kd, vd = state_hgrn.shape
    assert batch == 1 and dec_seq == 1 and kd == vd
    dh = hw // 2
    dims = dict(att_width=n_att_heads * hw, rec_width=n_rec_heads * kd, dh=dh, kd=kd)
    aw = dims["att_width"]

    w_in_b = w_in.astype(BF16)
    w_out_b = w_out.astype(BF16)
    w_up_b = w_up.astype(BF16)
    w_down_b = w_down.astype(BF16)
    pool_k = cache_k.reshape(depth * n_phys, page, n_att_heads, hw)
    pool_v = cache_v.reshape(depth * n_phys, page, n_att_heads, hw)

    hp = x_prompt.reshape(seq, d_model)
    hs = x_sample.reshape(dec_batch, d_model)
    outs = [[] for _ in range(6)]
    for l in range(depth):
        lam_init = 0.8 - 0.6 * math.exp(-0.3 * l)
        lam_rows = jnp.stack([lambda_q1[l], lambda_k1[l], lambda_q2[l], lambda_k2[l]])
        last = l == depth - 1

        (q, k, kb, v, vb), (qr, g, kk, vr, gate) = _mixer_inputs(
            hp, l, dims, norm1_w, w_in_b, lb_logits, bm=PROMPT_ROWS, bn=PROMPT_COLS)
        o_att = _prompt_attn(q, kb, vb, lam_rows, subln_w[l], n_heads=n_att_heads, dh=dh, lam_init=lam_init,
                             bq=PROMPT_Q_BLOCK)
        o_rec, s_p = _prompt_hgrn(qr, g, kk, vr, gate, rec_norm_w[l], chunk=HGRN_CHUNK, bt=HGRN_STEP)
        h = _out_proj(hp, o_att, o_rec, w_out_b[l], bm=FFN_ROWS, bn=OUT_COLS)
        hp = _ffn(h, norm2_w[l], w_up_b[l], w_down_b[l], final_norm_w, final_norm=last, bm=FFN_ROWS, bf=FFN_COLS)
        outs[0].append(k.reshape(batch, seq, n_att_heads, hw))
        outs[1].append(v.reshape(batch, seq, n_att_heads, hw))
        outs[2].append(s_p.reshape(batch, n_rec_heads, kd, vd))

        (q, k, _, v, _), (qr, g, kk, vr, gate) = _mixer_inputs(
            hs, l, dims, norm1_w, w_in_b, lb_logits, bm=dec_batch, bn=2 * PROMPT_COLS)
        o_att = _sample_attn(q, k, v, pool_k, pool_v, page_table + l * n_phys, lam_rows, subln_w[l],
                             n_heads=n_att_heads, dh=dh, lam_init=lam_init, pages_per_step=SAMPLE_PAGES_PER_STEP)
        o_rec, s_s = _sample_hgrn(qr, g, kk, vr, gate, rec_norm_w[l], state_hgrn[l], bb=SAMPLE_HGRN_SEQS)
        h = _out_proj(hs, o_att, o_rec, w_out_b[l], bm=dec_batch, bn=OUT_COLS)
        hs = _ffn(h, norm2_w[l], w_up_b[l], w_down_b[l], final_norm_w, final_norm=last, bm=dec_batch,
                  bf=2 * FFN_COLS)
        outs[3].append(k.reshape(dec_batch, dec_seq, n_att_heads, hw))
        outs[4].append(v.reshape(dec_batch, dec_seq, n_att_heads, hw))
        outs[5].append(s_s)

    return (hp.reshape(batch, seq, d_model), hs.reshape(dec_batch, dec_seq, d_model),
            jnp.stack(outs[0]), jnp.stack(outs[1]), jnp.stack(outs[2]),
            jnp.stack(outs[3]), jnp.stack(outs[4]), jnp.stack(outs[5]))
```

```python
import functools
import math

import jax
import jax.numpy as jnp
from jax import lax
from jax.experimental import pallas as pl
from jax.experimental.pallas import tpu as pltpu

F32 = jnp.float32
BF16 = jnp.bfloat16
EPS = 1e-6
LANES = 128
SUBLANES = 8
VMEM_CAP_BYTES = 60000 * 1024
NEG_INF = float("-inf")


def _cparams(semantics, vmem_bytes):
    return pltpu.CompilerParams(dimension_semantics=semantics,
                                vmem_limit_bytes=int(min(VMEM_CAP_BYTES, vmem_bytes)))


def _dot(a, b):
    return jnp.dot(a, b, preferred_element_type=F32)


def _dot_nt(a, b):
    return lax.dot_general(a, b, (((1,), (1,)), ((), ())), preferred_element_type=F32)


def _dot_tn(a, b):
    return lax.dot_general(a, b, (((0,), (0,)), ((), ())), preferred_element_type=F32)


def _sigmoid(x):
    return 1.0 / (1.0 + jnp.exp(-x))


def _silu(x):
    return x * _sigmoid(x)


def _rmsnorm_kernel(x_ref, w_ref, o_ref):
    x = x_ref[...]
    ms = jnp.mean(x * x, axis=-1, keepdims=True)
    o_ref[...] = (x * lax.rsqrt(ms + EPS) * w_ref[...]).astype(o_ref.dtype)


def _rmsnorm(x, w, *, bm):
    m, d = x.shape
    return pl.pallas_call(
        _rmsnorm_kernel,
        grid=(m // bm,),
        in_specs=[pl.BlockSpec((bm, d), lambda i: (i, 0)),
                  pl.BlockSpec((1, d), lambda i: (0, 0))],
        out_specs=pl.BlockSpec((bm, d), lambda i: (i, 0)),
        out_shape=jax.ShapeDtypeStruct((m, d), BF16),
        compiler_params=_cparams(("parallel",), 2 * bm * d * 6 + 2 * bm * d * 4 + (2 << 20)),
        name="rmsnorm",
    )(x, w.reshape(1, d))


def _wdot(x, w_ref):
    return _dot(x, w_ref[...].astype(BF16))


def _proj_specs(d, bm, bn, first_block, n_groups, groups_stride):
    xspec = pl.BlockSpec((bm, d), lambda j, i: (i, 0))
    wspecs = [pl.BlockSpec((d, bn), lambda j, i, g=g: (0, first_block + g * groups_stride + j))
              for g in range(n_groups)]
    return xspec, wspecs


def _proj_att_kernel(x_ref, wq_ref, wk_ref, wv_ref, q_ref, k_ref, kb_ref, v_ref, vb_ref, *, scale):
    x = x_ref[...]
    q_ref[...] = (_wdot(x, wq_ref) * scale).astype(q_ref.dtype)
    k = _wdot(x, wk_ref)
    k_ref[...] = k
    kb_ref[...] = k.astype(kb_ref.dtype)
    v = _wdot(x, wv_ref)
    v_ref[...] = v
    vb_ref[...] = v.astype(vb_ref.dtype)


def _proj_att(xn, w_in, *, att_width, scale, bm, bn):
    m, d = xn.shape
    nb = att_width // bn
    xspec, wspecs = _proj_specs(d, bm, bn, 0, 3, nb)
    ospec = pl.BlockSpec((bm, bn), lambda j, i: (i, j))
    f32o = jax.ShapeDtypeStruct((m, att_width), F32)
    b16o = jax.ShapeDtypeStruct((m, att_width), BF16)
    vmem = 2 * (bm * d * 2 + 3 * d * bn * 4 + bm * bn * 14) + 3 * d * bn * 2 + 6 * bm * bn * 4 + (2 << 20)
    return pl.pallas_call(
        functools.partial(_proj_att_kernel, scale=scale),
        grid=(nb, m // bm),
        in_specs=[xspec] + wspecs,
        out_specs=[ospec] * 5,
        out_shape=[b16o, f32o, b16o, f32o, b16o],
        compiler_params=_cparams(("parallel", "arbitrary"), vmem),
        name="proj_att",
    )(xn, w_in, w_in, w_in)


def _store_heads(o_ref, x, hd):
    for h in range(o_ref.shape[0]):
        o_ref[h] = x[:, h * hd:(h + 1) * hd]


def _proj_rec_qf_kernel(x_ref, wq_ref, wf_ref, lb_ref, q_ref, g_ref, k_ref, *, hd, layer):
    x = x_ref[...]
    lbl = lb_ref[...]
    e = jnp.exp(lbl - jnp.max(lbl, axis=0, keepdims=True))
    lb = jnp.sum(e[:layer + 1, :], axis=0, keepdims=True) / jnp.sum(e, axis=0, keepdims=True)
    f = lb + (1.0 - lb) * _sigmoid(_wdot(x, wf_ref))
    _store_heads(q_ref, _silu(_wdot(x, wq_ref)), hd)
    _store_heads(g_ref, jnp.log(f), hd)
    _store_heads(k_ref, 1.0 - f, hd)


def _proj_rec_vg_kernel(x_ref, wi_ref, wg_ref, v_ref, t_ref, *, hd):
    x = x_ref[...]
    _store_heads(v_ref, _wdot(x, wi_ref), hd)
    _store_heads(t_ref, _silu(_wdot(x, wg_ref)), hd)


def _proj_rec(xn, w_in, lb_logits, *, layer, col0, rec_width, hd, bm, bn):
    m, d = xn.shape
    nb = rec_width // bn
    c0 = col0 // bn
    hb = bn // hd
    lbspec = pl.BlockSpec((lb_logits.shape[0], bn), lambda j, i: (0, j))
    ospec = pl.BlockSpec((hb, bm, hd), lambda j, i: (j, i, 0))
    oshape = jax.ShapeDtypeStruct((rec_width // hd, m, hd), F32)
    params = _cparams(("parallel", "arbitrary"),
                      2 * (bm * d * 2 + 2 * d * bn * 4 + 3 * bm * bn * 4) + 2 * d * bn * 2 + 8 * bm * bn * 4
                      + (2 << 20))
    xspec, wspecs = _proj_specs(d, bm, bn, c0, 2, nb)
    qr, g, kk = pl.pallas_call(
        functools.partial(_proj_rec_qf_kernel, hd=hd, layer=layer),
        grid=(nb, m // bm),
        in_specs=[xspec] + wspecs + [lbspec],
        out_specs=[ospec] * 3,
        out_shape=[oshape] * 3,
        compiler_params=params,
        name="proj_rec_qf",
    )(xn, w_in, w_in, lb_logits)
    xspec, wspecs = _proj_specs(d, bm, bn, c0 + 2 * nb, 2, nb)
    vr, gate = pl.pallas_call(
        functools.partial(_proj_rec_vg_kernel, hd=hd),
        grid=(nb, m // bm),
        in_specs=[xspec] + wspecs,
        out_specs=[ospec] * 2,
        out_shape=[oshape] * 2,
        compiler_params=params,
        name="proj_rec_vg",
    )(xn, w_in, w_in)
    return qr, g, kk, vr, gate


def _out_proj_kernel(x_ref, a_ref, r_ref, wa_ref, wr_ref, h_ref):
    h_ref[...] = x_ref[...] + _dot(a_ref[...], wa_ref[...]) + _dot(r_ref[...], wr_ref[...])


def _out_proj(x, o_att, o_rec, w_out, *, bm, bn):
    m, d = x.shape
    ka = o_att.shape[1]
    kr = o_rec.shape[1]
    assert ka == kr
    vmem = 2 * (bm * bn * 8 + bm * (ka + kr) * 2 + (ka + kr) * bn * 2) + 2 * bm * bn * 4 + (2 << 20)
    return pl.pallas_call(
        _out_proj_kernel,
        grid=(m // bm, d // bn),
        in_specs=[pl.BlockSpec((bm, bn), lambda i, j: (i, j)),
                  pl.BlockSpec((bm, ka), lambda i, j: (i, 0)),
                  pl.BlockSpec((bm, kr), lambda i, j: (i, 0)),
                  pl.BlockSpec((ka, bn), lambda i, j: (0, j)),
                  pl.BlockSpec((kr, bn), lambda i, j: (1, j))],
        out_specs=pl.BlockSpec((bm, bn), lambda i, j: (i, j)),
        out_shape=jax.ShapeDtypeStruct((m, d), F32),
        compiler_params=_cparams(("parallel", "arbitrary"), vmem),
        name="out_proj",
    )(x, o_att, o_rec, w_out, w_out)


def _ffn_kernel(h_ref, n2_ref, wu_ref, wd_ref, nf_ref, y_ref, *rest, final_norm, emit_weights):
    if emit_weights:
        wub_ref, wdb_ref, hn_ref = rest
    else:
        (hn_ref,) = rest
    f = pl.program_id(1)

    @pl.when(f == 0)
    def _():
        h = h_ref[...]
        ms = jnp.mean(h * h, axis=-1, keepdims=True)
        hn_ref[...] = (h * lax.rsqrt(ms + EPS) * n2_ref[...]).astype(hn_ref.dtype)
        y_ref[...] = h

    wu = wu_ref[...].astype(BF16)
    wd = wd_ref[...].astype(BF16)
    if emit_weights:
        wub_ref[...] = wu
        wdb_ref[...] = wd
    u = jnp.maximum(_dot(hn_ref[...], wu), 0.0)
    y_ref[...] += _dot((u * u).astype(BF16), wd)

    if final_norm:
        @pl.when(f == pl.num_programs(1) - 1)
        def _():
            y = y_ref[...]
            ms = jnp.mean(y * y, axis=-1, keepdims=True)
            y_ref[...] = y * lax.rsqrt(ms + EPS) * nf_ref[...]


def _ffn(h, norm2_w, w_up, w_down, final_w, *, final_norm, emit_weights, bm, bf):
    m, d = h.shape
    dff = w_up.shape[1]
    wbytes = w_up.dtype.itemsize
    uspec = pl.BlockSpec((d, bf), lambda i, f: (0, f))
    dspec = pl.BlockSpec((bf, d), lambda i, f: (f, 0))
    yspec = pl.BlockSpec((bm, d), lambda i, f: (i, 0))
    yshape = jax.ShapeDtypeStruct((m, d), F32)
    vmem = (bm * d * 4 + 2 * bm * d * 4 + bm * d * 2 + 2 * 2 * d * bf * wbytes + bm * bf * 8 + bm * d * 4
            + 2 * d * bf * 2 + (2 << 20))
    if emit_weights:
        assert m == bm, "every weight tile must be visited exactly once"
        out_specs = [yspec, uspec, dspec]
        out_shape = [yshape, jax.ShapeDtypeStruct(w_up.shape, BF16), jax.ShapeDtypeStruct(w_down.shape, BF16)]
        vmem += 2 * 2 * d * bf * 2
    else:
        out_specs = yspec
        out_shape = yshape
    return pl.pallas_call(
        functools.partial(_ffn_kernel, final_norm=final_norm, emit_weights=emit_weights),
        grid=(m // bm, dff // bf),
        in_specs=[pl.BlockSpec((bm, d), lambda i, f: (i, 0), pipeline_mode=pl.Buffered(1)),
                  pl.BlockSpec((1, d), lambda i, f: (0, 0)),
                  uspec, dspec,
                  pl.BlockSpec((1, d), lambda i, f: (0, 0))],
        out_specs=out_specs,
        out_shape=out_shape,
        scratch_shapes=[pltpu.VMEM((bm, d), BF16)],
        compiler_params=_cparams(("parallel", "arbitrary"), vmem),
        name="ffn",
    )(h, norm2_w.reshape(1, d), w_up, w_down, final_w.reshape(1, d))


def _lambda_full(lam_ref, lam_init):
    lam = lam_ref[...]
    s1 = jnp.sum(lam[0:1, :] * lam[1:2, :], axis=-1, keepdims=True)
    s2 = jnp.sum(lam[2:3, :] * lam[3:4, :], axis=-1, keepdims=True)
    return jnp.exp(s1) - jnp.exp(s2) + lam_init


def _diff_combine(acc0, l0, acc1, l1, lam, sub_w, out_scale):
    o = acc0 / l0 - lam * (acc1 / l1)
    ms = jnp.mean(o * o, axis=-1, keepdims=True)
    return o * lax.rsqrt(ms + EPS) * sub_w * out_scale


def _lane_tile(x, width):
    return jnp.concatenate([x] * (width // LANES), axis=1)


def _prompt_attn_kernel(lam_ref, sub_ref, q_ref, k_ref, v_ref, o_ref, qs_ref, s0_ref, s1_ref, p0_ref, p1_ref,
                        a0_ref, a1_ref, m_ref, l_ref, acc_ref, *, bq, dh, lam_init):
    qi = pl.program_id(1)
    q = q_ref[...]
    zeros = jnp.zeros((bq, dh), q.dtype)
    qs_ref[:bq, :] = jnp.concatenate([q[:, :dh], zeros], axis=1)
    qs_ref[bq:, :] = jnp.concatenate([zeros, q[:, dh:]], axis=1)
    m_ref[...] = jnp.full(m_ref.shape, NEG_INF, F32)
    l_ref[...] = jnp.zeros(l_ref.shape, F32)
    acc_ref[...] = jnp.zeros(acc_ref.shape, F32)

    def keys(j):
        return pl.ds(pl.multiple_of(j * bq, bq), bq)

    def scores(j, s_ref):
        s_ref[...] = _dot_nt(qs_ref[...], k_ref[keys(j), :])

    def softmax(s_ref, p_ref, a_ref, diagonal):
        s = s_ref[...]
        if diagonal:
            row = lax.broadcasted_iota(jnp.int32, s.shape, 0) % bq
            col = lax.broadcasted_iota(jnp.int32, s.shape, 1)
            s = jnp.where(row >= col, s, NEG_INF)
        m_prev = m_ref[...]
        m_new = jnp.maximum(m_prev, jnp.max(s, axis=1, keepdims=True))
        alpha = jnp.exp(m_prev - m_new)
        p = jnp.exp(s - _lane_tile(m_new, s.shape[1]))
        psum = p[:, :LANES]
        for t in range(1, s.shape[1] // LANES):
            psum = psum + p[:, t * LANES:(t + 1) * LANES]
        l_ref[...] = alpha * l_ref[...] + psum
        m_ref[...] = m_new
        a_ref[...] = alpha
        p_ref[...] = p.astype(p_ref.dtype)

    def mix(j, p_ref, a_ref):
        acc_ref[...] = (_lane_tile(a_ref[...], acc_ref.shape[1]) * acc_ref[...]
                        + _dot(p_ref[...], v_ref[keys(j), :]))

    scores(0, s0_ref)

    def pair(t, carry):
        j = 2 * t
        scores(j + 1, s1_ref)
        softmax(s0_ref, p0_ref, a0_ref, False)
        mix(j, p0_ref, a0_ref)
        softmax(s1_ref, p1_ref, a1_ref, False)
        scores(j + 2, s0_ref)
        mix(j + 1, p1_ref, a1_ref)
        return carry

    lax.fori_loop(0, qi // 2, pair, 0)
    j = 2 * (qi // 2)

    @pl.when(qi % 2 == 0)
    def _():
        softmax(s0_ref, p0_ref, a0_ref, True)
        mix(j, p0_ref, a0_ref)

    @pl.when(qi % 2 == 1)
    def _():
        scores(j + 1, s1_ref)
        softmax(s0_ref, p0_ref, a0_ref, False)
        mix(j, p0_ref, a0_ref)
        softmax(s1_ref, p1_ref, a1_ref, True)
        mix(j + 1, p1_ref, a1_ref)

    l = jnp.sum(l_ref[...], axis=1, keepdims=True)
    acc = acc_ref[...]
    lam = _lambda_full(lam_ref, lam_init)
    o = _diff_combine(acc[:bq], l[:bq], acc[bq:], l[bq:], lam, sub_ref[...], 1.0 - lam_init)
    o_ref[...] = o.astype(o_ref.dtype)


def _prompt_attn(q, kb, vb, lam_rows, subln_w, *, n_heads, dh, lam_init, bq):
    m, width = q.shape
    hw = 2 * dh
    rows = 2 * bq
    assert width == n_heads * hw and bq % LANES == 0
    scratch = [pltpu.VMEM((rows, hw), BF16),
               pltpu.VMEM((rows, bq), F32), pltpu.VMEM((rows, bq), F32),
               pltpu.VMEM((rows, bq), BF16), pltpu.VMEM((rows, bq), BF16),
               pltpu.VMEM((rows, LANES), F32), pltpu.VMEM((rows, LANES), F32),
               pltpu.VMEM((rows, LANES), F32), pltpu.VMEM((rows, LANES), F32),
               pltpu.VMEM((rows, hw), F32)]
    scratch_bytes = rows * (hw * 2 + 2 * bq * 4 + 2 * bq * 2 + 4 * LANES * 4 + hw * 4)
    vmem = 2 * (2 * m * hw * 2 + 2 * bq * hw * 2) + scratch_bytes + 3 * rows * bq * 4 + (2 << 20)
    return pl.pallas_call(
        functools.partial(_prompt_attn_kernel, bq=bq, dh=dh, lam_init=lam_init),
        grid=(n_heads, m // bq),
        in_specs=[pl.BlockSpec((4, dh), lambda h, i: (0, 0)),
                  pl.BlockSpec((1, hw), lambda h, i: (0, 0)),
                  pl.BlockSpec((bq, hw), lambda h, i: (i, h)),
                  pl.BlockSpec((m, hw), lambda h, i: (0, h)),
                  pl.BlockSpec((m, hw), lambda h, i: (0, h))],
        out_specs=pl.BlockSpec((bq, hw), lambda h, i: (i, h)),
        out_shape=jax.ShapeDtypeStruct((m, width), BF16),
        scratch_shapes=scratch,
        compiler_params=_cparams(("parallel", "arbitrary"), vmem),
        name="prompt_attn",
    )(lam_rows, subln_w.reshape(1, hw), q, kb, vb)


def _split3_bf16(x):
    hi = x.astype(BF16)
    r1 = x - hi.astype(F32)
    mid = r1.astype(BF16)
    lo = (r1 - mid.astype(F32)).astype(BF16)
    return hi, mid, lo


def _group_row(x, group, row):
    n, lanes = x.shape
    x3 = x.reshape(n // group, group, lanes)
    return jnp.broadcast_to(x3[:, row:row + 1, :], x3.shape).reshape(n, lanes)


def _hgrn_chunk(qr, g, kk, v, st, tril, dsum):
    c, kd = g.shape
    vb = v.astype(BF16)
    g3 = jnp.concatenate(_split3_bf16(g), axis=1)
    b3 = _dot(tril, g3)
    b = b3[:, :kd] + b3[:, kd:2 * kd] + b3[:, 2 * kd:]
    b_last = b[c - 1:c, :]
    row = lax.broadcasted_iota(jnp.int32, (c, kd), 0)
    trow = lax.broadcasted_iota(jnp.int32, (c, c), 0)
    tcol = lax.broadcasted_iota(jnp.int32, (c, c), 1)

    o = _dot_nt((qr * jnp.exp(b)).astype(BF16), st.astype(BF16))
    k_state = (kk * jnp.exp(b_last - b)).astype(BF16)
    st_new = jnp.exp(b_last) * st + _dot_tn(vb, k_state)

    rmod = row % SUBLANES
    parts = []
    for j in range(SUBLANES):
        bj = _group_row(b, SUBLANES, j)
        kj = _group_row(kk, SUBLANES, j)
        pj = qr * kj * jnp.exp(jnp.minimum(b - bj, 0.0))
        parts.append(jnp.where(rmod >= j, pj, 0.0).astype(BF16))
    a = jnp.where(trow // SUBLANES == tcol // SUBLANES, _dot(jnp.concatenate(parts, axis=1), dsum), 0.0)

    m = SUBLANES
    while m < c:
        e = jnp.exp(-jnp.abs(b - _group_row(b, 2 * m, m - 1)))
        right = (row % (2 * m)) >= m
        qm = jnp.where(right, qr * e, 0.0).astype(BF16)
        km = jnp.where(right, 0.0, kk * e).astype(BF16)
        a = a + jnp.where(trow // (2 * m) == tcol // (2 * m), _dot_nt(qm, km), 0.0)
        m *= 2

    o = o + _dot(a.astype(BF16), vb)
    return o, st_new


def _prompt_hgrn_kernel(q_ref, g_ref, k_ref, v_ref, t_ref, nw_ref, tril_ref, dsum_ref, o_ref, s_ref, st_ref, *,
                        chunk):
    ci = pl.program_id(1)

    @pl.when(ci == 0)
    def _():
        st_ref[...] = jnp.zeros(st_ref.shape, F32)

    tril = tril_ref[...]
    dsum = dsum_ref[...]
    st = st_ref[...]
    for sub in range(q_ref.shape[0] // chunk):
        rows = slice(sub * chunk, (sub + 1) * chunk)
        o, st = _hgrn_chunk(q_ref[rows, :], g_ref[rows, :], k_ref[rows, :], v_ref[rows, :], st, tril, dsum)
        ms = jnp.mean(o * o, axis=-1, keepdims=True)
        o_ref[rows, :] = (o * lax.rsqrt(ms + EPS) * nw_ref[...] * t_ref[rows, :]).astype(o_ref.dtype)
    st_ref[...] = st

    @pl.when(ci == pl.num_programs(1) - 1)
    def _():
        s_ref[...] = st.T


def _prompt_hgrn(qr, g, kk, vr, gate, rec_norm_w, *, chunk, bt):
    nh, m, kd = qr.shape
    assert vr.shape[2] == kd
    tril = jnp.tril(jnp.ones((chunk, chunk), F32)).astype(BF16)
    dsum = (jnp.arange(SUBLANES * kd)[:, None] // kd == jnp.arange(chunk)[None, :] % SUBLANES).astype(BF16)
    ispec = pl.BlockSpec((None, bt, kd), lambda h, c: (h, c, 0))
    vmem = 2 * (5 * bt * kd * 4 + bt * kd * 2) + 64 * chunk * kd * 4 + (4 << 20)
    return pl.pallas_call(
        functools.partial(_prompt_hgrn_kernel, chunk=chunk),
        grid=(nh, m // bt),
        in_specs=[ispec] * 5 + [pl.BlockSpec((1, kd), lambda h, c: (0, 0)),
                                pl.BlockSpec((chunk, chunk), lambda h, c: (0, 0)),
                                pl.BlockSpec((SUBLANES * kd, chunk), lambda h, c: (0, 0))],
        out_specs=[pl.BlockSpec((bt, kd), lambda h, c: (c, h)),
                   pl.BlockSpec((None, kd, kd), lambda h, c: (h, 0, 0))],
        out_shape=[jax.ShapeDtypeStruct((m, nh * kd), BF16),
                   jax.ShapeDtypeStruct((nh, kd, kd), F32)],
        scratch_shapes=[pltpu.VMEM((kd, kd), F32)],
        compiler_params=_cparams(("parallel", "arbitrary"), vmem),
        name="prompt_hgrn",
    )(qr, g, kk, vr, gate, rec_norm_w.reshape(1, kd), tril, dsum)


def _swap_halves(x, dh):
    return jnp.concatenate([x[..., dh:], x[..., :dh]], axis=-1)


def _sample_attn_kernel(pt_ref, lam_ref, sub_ref, ones_ref, q_ref, ks_ref, vs_ref, *rest, n_pages_step, dh,
                        lam_init):
    del pt_ref
    k_refs = rest[:n_pages_step]
    v_refs = rest[n_pages_step:2 * n_pages_step]
    o_ref, m_ref, l_ref, same_ref, cross_ref = rest[2 * n_pages_step:]
    step = pl.program_id(1)
    q = q_ref[...].astype(F32)

    @pl.when(step == 0)
    def _():
        m_ref[...] = jnp.full(m_ref.shape, NEG_INF, F32)
        l_ref[...] = jnp.zeros(l_ref.shape, F32)
        same_ref[...] = jnp.zeros(same_ref.shape, F32)
        cross_ref[...] = jnp.zeros(cross_ref.shape, F32)

    def half_sums(x):
        lead = x.shape[:-1]
        flat = x.reshape(-1, x.shape[-1]).astype(BF16)
        return _dot(flat, ones_ref[...]).reshape(*lead, x.shape[-1])

    def absorb(s, v):
        m_prev = m_ref[...]
        m_new = jnp.maximum(m_prev, jnp.max(s, axis=0))
        alpha = jnp.exp(m_prev - m_new)
        p = jnp.exp(s - m_new)
        m_ref[...] = m_new
        l_ref[...] = alpha * l_ref[...] + jnp.sum(p, axis=0)
        same_ref[...] = alpha * same_ref[...] + jnp.sum(p * v, axis=0)
        cross_ref[...] = alpha * cross_ref[...] + jnp.sum(p * _swap_halves(v, dh), axis=0)

    for g in range(n_pages_step):
        absorb(half_sums(k_refs[g][...] * q), v_refs[g][...])

    @pl.when(step == pl.num_programs(1) - 1)
    def _():
        absorb(half_sums((ks_ref[...] * q)[None]), vs_ref[...][None])
        same = same_ref[...]
        cross = cross_ref[...]
        l = l_ref[...]
        acc0 = jnp.concatenate([same[:, :dh], cross[:, :dh]], axis=-1)
        acc1 = jnp.concatenate([cross[:, dh:], same[:, dh:]], axis=-1)
        l0 = jnp.concatenate([l[:, :dh], l[:, :dh]], axis=-1)
        l1 = jnp.concatenate([l[:, dh:], l[:, dh:]], axis=-1)
        lam = _lambda_full(lam_ref, lam_init)
        o = _diff_combine(acc0, l0, acc1, l1, lam, sub_ref[...], 1.0 - lam_init)
        o_ref[...] = o.astype(o_ref.dtype)


def _sample_attn(q, k_self, v_self, pool_k, pool_v, page_ids, lam_rows, subln_w, *, n_heads, dh, lam_init,
                 pages_per_step):
    nb, width = q.shape
    n_pages = page_ids.shape[1]
    page = pool_k.shape[1]
    hw = 2 * dh
    g = pages_per_step
    assert n_pages % g == 0 and pool_k.shape[2:] == (n_heads, hw) and width == n_heads * hw
    ones = (jnp.arange(hw)[:, None] // dh == jnp.arange(hw)[None, :] // dh).astype(BF16)

    def page_spec(i):
        return pl.BlockSpec((None, page, n_heads, hw), lambda b, p, pt: (pt[b, p * g + i], 0, 0, 0))

    row_spec = pl.BlockSpec((None, n_heads, hw), lambda b, p, pt: (b, 0, 0))
    in_specs = ([pl.BlockSpec((4, dh), lambda b, p, pt: (0, 0)),
                 pl.BlockSpec((1, hw), lambda b, p, pt: (0, 0)),
                 pl.BlockSpec((hw, hw), lambda b, p, pt: (0, 0)),
                 row_spec, row_spec, row_spec]
                + [page_spec(i) for i in range(g)] * 2)
    page_bytes = page * width * 4
    vmem = 2 * 2 * g * page_bytes + 8 * page_bytes + (4 << 20)
    scratch = [pltpu.VMEM((n_heads, hw), F32)] * 4
    out = pl.pallas_call(
        functools.partial(_sample_attn_kernel, n_pages_step=g, dh=dh, lam_init=lam_init),
        grid_spec=pltpu.PrefetchScalarGridSpec(
            num_scalar_prefetch=1,
            grid=(nb, n_pages // g),
            in_specs=in_specs,
            out_specs=row_spec,
            scratch_shapes=scratch),
        out_shape=jax.ShapeDtypeStruct((nb, n_heads, hw), BF16),
        compiler_params=_cparams(("parallel", "arbitrary"), vmem),
        name="sample_attn",
    )(page_ids, lam_rows, subln_w.reshape(1, hw), ones, q.reshape(nb, n_heads, hw),
      k_self.reshape(nb, n_heads, hw), v_self.reshape(nb, n_heads, hw), *([pool_k] * g), *([pool_v] * g))
    return out.reshape(nb, width)


def _hi_lo(x):
    hi = x.astype(BF16).astype(F32)
    return hi, x - hi


def _sample_hgrn_kernel(q_ref, g_ref, k_ref, v_ref, t_ref, nw_ref, s_ref, o_ref, so_ref, orow_ref):
    nh, nb, kd = q_ref.shape
    f_t = jnp.exp(g_ref[...]).reshape(nh * nb, kd).T
    own = (lax.broadcasted_iota(jnp.int32, (nb, nb * kd), 1) // kd
           == lax.broadcasted_iota(jnp.int32, (nb, nb * kd), 0))

    def own_block(x):
        return jnp.where(own, jnp.concatenate([x] * nb, axis=1), 0.0)

    for h in range(nh):
        k_hi, k_lo = _hi_lo(k_ref[h])
        v_hi, v_lo = _hi_lo(v_ref[h])
        ka = jnp.concatenate([k_hi, k_hi, k_lo, jnp.zeros_like(k_lo)], axis=0).astype(BF16)
        vb = jnp.concatenate([own_block(v_hi), own_block(v_lo), own_block(v_hi),
                              jnp.zeros((nb, nb * kd), F32)], axis=0).astype(BF16)
        kv = _dot_tn(ka, vb)
        s_new = []
        for i in range(nb):
            r = h * nb + i
            s_i = f_t[:, r:r + 1] * s_ref[i, h] + kv[:, i * kd:(i + 1) * kd]
            so_ref[i, h] = s_i
            s_new.append(s_i.astype(BF16))
        q_rows = jnp.concatenate([q_ref[h], jnp.zeros((nb, kd), F32)], axis=0).astype(BF16)
        o_all = _dot(q_rows, jnp.concatenate(s_new, axis=1))
        o = jnp.concatenate([o_all[i:i + 1, i * kd:(i + 1) * kd] for i in range(nb)], axis=0)
        ms = jnp.mean(o * o, axis=-1, keepdims=True)
        orow_ref[:, h * kd:(h + 1) * kd] = o * lax.rsqrt(ms + EPS) * nw_ref[...] * t_ref[h]
    o_ref[...] = orow_ref[...].astype(o_ref.dtype)


def _sample_hgrn(qr, g, kk, vr, gate, rec_norm_w, state, *, bb):
    nh, nb, kd = qr.shape
    vd = vr.shape[2]
    assert state.shape == (nb, nh, kd, vd) and kd == vd
    ispec = pl.BlockSpec((nh, bb, kd), lambda b: (0, b, 0))
    sspec = pl.BlockSpec((bb, nh, kd, vd), lambda b: (b, 0, 0, 0))
    vmem = 2 * 2 * bb * nh * kd * vd * 4 + 2 * 5 * nh * bb * kd * 4 + 8 * nh * bb * kd * 4 + (4 << 20)
    return pl.pallas_call(
        _sample_hgrn_kernel,
        grid=(nb // bb,),
        in_specs=[ispec] * 5 + [pl.BlockSpec((1, vd), lambda b: (0, 0)), sspec],
        out_specs=[pl.BlockSpec((bb, nh * vd), lambda b: (b, 0)), sspec],
        out_shape=[jax.ShapeDtypeStruct((nb, nh * vd), BF16),
                   jax.ShapeDtypeStruct(state.shape, F32)],
        scratch_shapes=[pltpu.VMEM((bb, nh * vd), F32)],
        compiler_params=_cparams(("parallel",), vmem),
        name="sample_hgrn",
    )(qr, g, kk, vr, gate, rec_norm_w.reshape(1, vd), state)


ATT_PROJ_ROWS = 512
REC_PROJ_ROWS = 1024
PROJ_COLS = 256
PROMPT_Q_BLOCK = 512
HGRN_CHUNK = 128
HGRN_STEP = 512
FFN_ROWS = 512
FFN_COLS = 512
SAMPLE_FFN_COLS = 256
OUT_COLS = 1024
SAMPLE_PAGES_PER_STEP = 8
SAMPLE_HGRN_SEQS = 8


def _mixer_inputs(x, l, dims, norm1_w, w_in, lb_logits):
    m = x.shape[0]
    xn = _rmsnorm(x, norm1_w[l], bm=min(m, 256))
    att = _proj_att(xn, w_in[l], att_width=dims["att_width"], scale=dims["dh"] ** -0.5,
                    bm=min(m, ATT_PROJ_ROWS), bn=PROJ_COLS)
    rec = _proj_rec(xn, w_in[l], lb_logits, layer=l, col0=3 * dims["att_width"], rec_width=dims["rec_width"],
                    hd=dims["kd"], bm=min(m, REC_PROJ_ROWS), bn=PROJ_COLS)
    return att, rec


def kernel(x_prompt, x_sample, cache_k, cache_v, state_hgrn, page_table, norm1_w, w_in, lambda_q1, lambda_k1,
           lambda_q2, lambda_k2, subln_w, lb_logits, rec_norm_w, w_out, norm2_w, w_up, w_down, final_norm_w):
    batch, seq, d_model = x_prompt.shape
    dec_batch, dec_seq, _ = x_sample.shape
    depth, n_phys, page, n_att_heads, hw = cache_k.shape
    _, _, n_rec_heads, kd, vd = state_hgrn.shape
    assert batch == 1 and dec_seq == 1 and kd == vd
    dh = hw // 2
    dims = dict(att_width=n_att_heads * hw, rec_width=n_rec_heads * kd, dh=dh, kd=kd)
    aw = dims["att_width"]

    w_out_b = w_out.astype(BF16)
    pool_k = cache_k.reshape(depth * n_phys, page, n_att_heads, hw)
    pool_v = cache_v.reshape(depth * n_phys, page, n_att_heads, hw)

    hp = x_prompt.reshape(seq, d_model)
    hs = x_sample.reshape(dec_batch, d_model)
    outs = [[] for _ in range(6)]
    for l in range(depth):
        lam_init = 0.8 - 0.6 * math.exp(-0.3 * l)
        lam_rows = jnp.stack([lambda_q1[l], lambda_k1[l], lambda_q2[l], lambda_k2[l]])
        last = l == depth - 1

        (q, k, _, v, _), (qr, g, kk, vr, gate) = _mixer_inputs(hs, l, dims, norm1_w, w_in, lb_logits)
        o_att = _sample_attn(q, k, v, pool_k, pool_v, page_table + l * n_phys, lam_rows, subln_w[l],
                             n_heads=n_att_heads, dh=dh, lam_init=lam_init, pages_per_step=SAMPLE_PAGES_PER_STEP)
        o_rec, s_s = _sample_hgrn(qr, g, kk, vr, gate, rec_norm_w[l], state_hgrn[l], bb=SAMPLE_HGRN_SEQS)
        h = _out_proj(hs, o_att, o_rec, w_out_b[l], bm=dec_batch, bn=OUT_COLS)
        hs, w_up_b, w_down_b = _ffn(h, norm2_w[l], w_up[l], w_down[l], final_norm_w, final_norm=last,
                                    emit_weights=True, bm=dec_batch, bf=SAMPLE_FFN_COLS)
        outs[3].append(k.reshape(dec_batch, dec_seq, n_att_heads, hw))
        outs[4].append(v.reshape(dec_batch, dec_seq, n_att_heads, hw))
        outs[5].append(s_s)

        (q, k, kb, v, vb), (qr, g, kk, vr, gate) = _mixer_inputs(hp, l, dims, norm1_w, w_in, lb_logits)
        o_att = _prompt_attn(q, kb, vb, lam_rows, subln_w[l], n_heads=n_att_heads, dh=dh, lam_init=lam_init,
                             bq=PROMPT_Q_BLOCK)
        o_rec, s_p = _prompt_hgrn(qr, g, kk, vr, gate, rec_norm_w[l], chunk=HGRN_CHUNK, bt=HGRN_STEP)
        h = _out_proj(hp, o_att, o_rec, w_out_b[l], bm=FFN_ROWS, bn=OUT_COLS)
        hp = _ffn(h, norm2_w[l], w_up_b, w_down_b, final_norm_w, final_norm=last, emit_weights=False,
                  bm=FFN_ROWS, bf=FFN_COLS)
        outs[0].append(k.reshape(batch, seq, n_att_heads, hw))
        outs[1].append(v.reshape(batch, seq, n_att_heads, hw))
        outs[2].append(s_p.reshape(batch, n_rec_heads, kd, vd))

    return (hp.reshape(batch, seq, d_model), hs.reshape(dec_batch, dec_seq, d_model),
            jnp.stack(outs[0]), jnp.stack(outs[1]), jnp.stack(outs[2]),
            jnp.stack(outs[3]), jnp.stack(outs[4]), jnp.stack(outs[5]))
```

```python
import functools
import math

import jax
import jax.numpy as jnp
from jax import lax
from jax.experimental import pallas as pl
from jax.experimental.pallas import tpu as pltpu

F32 = jnp.float32
BF16 = jnp.bfloat16
EPS = 1e-6
LANES = 128
SUBLANES = 8
VMEM_CAP_BYTES = 60000 * 1024
NEG_INF = float("-inf")


def _cparams(semantics, vmem_bytes):
    return pltpu.CompilerParams(dimension_semantics=semantics,
                                vmem_limit_bytes=int(min(VMEM_CAP_BYTES, vmem_bytes)))


def _dot(a, b):
    return jnp.dot(a, b, preferred_element_type=F32)


def _dot_nt(a, b):
    return lax.dot_general(a, b, (((1,), (1,)), ((), ())), preferred_element_type=F32)


def _dot_tn(a, b):
    return lax.dot_general(a, b, (((0,), (0,)), ((), ())), preferred_element_type=F32)


def _sigmoid(x):
    return 1.0 / (1.0 + jnp.exp(-x))


def _silu(x):
    return x * _sigmoid(x)


def _rmsnorm_kernel(x_ref, w_ref, o_ref):
    x = x_ref[...]
    ms = jnp.mean(x * x, axis=-1, keepdims=True)
    o_ref[...] = (x * lax.rsqrt(ms + EPS) * w_ref[...]).astype(o_ref.dtype)


def _rmsnorm(x, w, *, bm):
    m, d = x.shape
    return pl.pallas_call(
        _rmsnorm_kernel,
        grid=(m // bm,),
        in_specs=[pl.BlockSpec((bm, d), lambda i: (i, 0)),
                  pl.BlockSpec((1, d), lambda i: (0, 0))],
        out_specs=pl.BlockSpec((bm, d), lambda i: (i, 0)),
        out_shape=jax.ShapeDtypeStruct((m, d), BF16),
        compiler_params=_cparams(("parallel",), 2 * bm * d * 6 + 2 * bm * d * 4 + (2 << 20)),
        name="rmsnorm",
    )(x, w.reshape(1, d))


def _proj_weight_specs(weights, d, bn, col0, width, n_groups):
    if isinstance(weights, (list, tuple)):
        assert len(weights) == n_groups
        return list(weights), [pl.BlockSpec((d, bn), lambda i, j: (0, j))] * n_groups
    specs = [pl.BlockSpec((d, bn), lambda i, j, g=g: (0, (col0 + g * width) // bn + j)) for g in range(n_groups)]
    return [weights] * n_groups, specs


def _load_weights(w_refs, wb_refs):
    ws = [w_ref[...].astype(BF16) for w_ref in w_refs]
    for w, wb_ref in zip(ws, wb_refs):
        wb_ref[...] = w
    return ws


def _proj_att_kernel(x_ref, *refs, scale, emit_weights):
    w_refs = refs[:3]
    q_ref, k_ref, kb_ref, v_ref, vb_ref = refs[3:8]
    wq, wk, wv = _load_weights(w_refs, refs[8:] if emit_weights else ())
    x = x_ref[...]
    q_ref[...] = (_dot(x, wq) * scale).astype(q_ref.dtype)
    k = _dot(x, wk)
    k_ref[...] = k
    kb_ref[...] = k.astype(kb_ref.dtype)
    v = _dot(x, wv)
    v_ref[...] = v
    vb_ref[...] = v.astype(vb_ref.dtype)


def _proj_att(xn, weights, *, col0, att_width, scale, emit_weights, bm, bn):
    m, d = xn.shape
    nb = att_width // bn
    assert not emit_weights or m == bm, "every weight tile must be visited exactly once"
    w_ops, wspecs = _proj_weight_specs(weights, d, bn, col0, att_width, 3)
    wbytes = w_ops[0].dtype.itemsize
    ospec = pl.BlockSpec((bm, bn), lambda i, j: (i, j))
    f32o = jax.ShapeDtypeStruct((m, att_width), F32)
    b16o = jax.ShapeDtypeStruct((m, att_width), BF16)
    out_specs = [ospec] * 5
    out_shape = [b16o, f32o, b16o, f32o, b16o]
    vmem = 2 * (bm * d * 2 + 3 * d * bn * wbytes + bm * bn * 14) + 6 * bm * bn * 4 + 3 * d * bn * 2 + (2 << 20)
    if emit_weights:
        out_specs += [pl.BlockSpec((d, bn), lambda i, j: (0, j))] * 3
        out_shape += [jax.ShapeDtypeStruct((d, att_width), BF16)] * 3
        vmem += 2 * 3 * d * bn * 2
    outs = pl.pallas_call(
        functools.partial(_proj_att_kernel, scale=scale, emit_weights=emit_weights),
        grid=(m // bm, nb),
        in_specs=[pl.BlockSpec((bm, d), lambda i, j: (i, 0))] + wspecs,
        out_specs=out_specs,
        out_shape=out_shape,
        compiler_params=_cparams(("parallel", "arbitrary"), vmem),
        name="proj_att",
    )(xn, *w_ops)
    return outs[:5], list(outs[5:])


def _store_heads(o_ref, x, hd):
    for h in range(o_ref.shape[0]):
        o_ref[h] = x[:, h * hd:(h + 1) * hd]


def _proj_rec_kernel(x_ref, *refs, hd, layer, emit_weights):
    w_refs = refs[:4]
    lb_ref = refs[4]
    q_ref, g_ref, k_ref, v_ref, t_ref = refs[5:10]
    wq, wf, wi, wg = _load_weights(w_refs, refs[10:] if emit_weights else ())
    x = x_ref[...]
    lbl = lb_ref[...]
    e = jnp.exp(lbl - jnp.max(lbl, axis=0, keepdims=True))
    lb = jnp.sum(e[:layer + 1, :], axis=0, keepdims=True) / jnp.sum(e, axis=0, keepdims=True)
    f = lb + (1.0 - lb) * _sigmoid(_dot(x, wf))
    _store_heads(q_ref, _silu(_dot(x, wq)), hd)
    _store_heads(g_ref, jnp.log2(f), hd)
    _store_heads(k_ref, 1.0 - f, hd)
    _store_heads(v_ref, _dot(x, wi), hd)
    _store_heads(t_ref, _silu(_dot(x, wg)), hd)


def _proj_rec(xn, weights, lb_logits, *, layer, col0, rec_width, hd, emit_weights, bm, bn):
    m, d = xn.shape
    nb = rec_width // bn
    hb = bn // hd
    assert not emit_weights or m == bm, "every weight tile must be visited exactly once"
    w_ops, wspecs = _proj_weight_specs(weights, d, bn, col0, rec_width, 4)
    wbytes = w_ops[0].dtype.itemsize
    lbspec = pl.BlockSpec((lb_logits.shape[0], bn), lambda i, j: (0, j))
    ospec = pl.BlockSpec((hb, bm, hd), lambda i, j: (j, i, 0))
    oshape = jax.ShapeDtypeStruct((rec_width // hd, m, hd), F32)
    out_specs = [ospec] * 5
    out_shape = [oshape] * 5
    vmem = 2 * (bm * d * 2 + 4 * d * bn * wbytes + 5 * bm * bn * 4) + 10 * bm * bn * 4 + 4 * d * bn * 2 + (2 << 20)
    if emit_weights:
        out_specs += [pl.BlockSpec((d, bn), lambda i, j: (0, j))] * 4
        out_shape += [jax.ShapeDtypeStruct((d, rec_width), BF16)] * 4
        vmem += 2 * 4 * d * bn * 2
    outs = pl.pallas_call(
        functools.partial(_proj_rec_kernel, hd=hd, layer=layer, emit_weights=emit_weights),
        grid=(m // bm, nb),
        in_specs=[pl.BlockSpec((bm, d), lambda i, j: (i, 0))] + wspecs + [lbspec],
        out_specs=out_specs,
        out_shape=out_shape,
        compiler_params=_cparams(("parallel", "arbitrary"), vmem),
        name="proj_rec",
    )(xn, *w_ops, lb_logits)
    return outs[:5], list(outs[5:])


def _out_proj_kernel(x_ref, a_ref, r_ref, wa_ref, wr_ref, h_ref):
    h_ref[...] = x_ref[...] + _dot(a_ref[...], wa_ref[...]) + _dot(r_ref[...], wr_ref[...])


def _out_proj(x, o_att, o_rec, w_out, *, bm, bn):
    m, d = x.shape
    ka = o_att.shape[1]
    kr = o_rec.shape[1]
    assert ka == kr
    vmem = 2 * (bm * bn * 8 + bm * (ka + kr) * 2 + (ka + kr) * bn * 2) + 2 * bm * bn * 4 + (2 << 20)
    return pl.pallas_call(
        _out_proj_kernel,
        grid=(m // bm, d // bn),
        in_specs=[pl.BlockSpec((bm, bn), lambda i, j: (i, j)),
                  pl.BlockSpec((bm, ka), lambda i, j: (i, 0)),
                  pl.BlockSpec((bm, kr), lambda i, j: (i, 0)),
                  pl.BlockSpec((ka, bn), lambda i, j: (0, j)),
                  pl.BlockSpec((kr, bn), lambda i, j: (1, j))],
        out_specs=pl.BlockSpec((bm, bn), lambda i, j: (i, j)),
        out_shape=jax.ShapeDtypeStruct((m, d), F32),
        compiler_params=_cparams(("parallel", "arbitrary"), vmem),
        name="out_proj",
    )(x, o_att, o_rec, w_out, w_out)


def _ffn_kernel(h_ref, n2_ref, wu_ref, wd_ref, nf_ref, y_ref, *rest, final_norm, emit_weights):
    if emit_weights:
        wub_ref, wdb_ref, hn_ref = rest
    else:
        (hn_ref,) = rest
    f = pl.program_id(1)

    @pl.when(f == 0)
    def _():
        h = h_ref[...]
        ms = jnp.mean(h * h, axis=-1, keepdims=True)
        hn_ref[...] = (h * lax.rsqrt(ms + EPS) * n2_ref[...]).astype(hn_ref.dtype)
        y_ref[...] = h

    wu = wu_ref[...].astype(BF16)
    wd = wd_ref[...].astype(BF16)
    if emit_weights:
        wub_ref[...] = wu
        wdb_ref[...] = wd
    u = jnp.maximum(_dot(hn_ref[...], wu), 0.0)
    y_ref[...] += _dot((u * u).astype(BF16), wd)

    if final_norm:
        @pl.when(f == pl.num_programs(1) - 1)
        def _():
            y = y_ref[...]
            ms = jnp.mean(y * y, axis=-1, keepdims=True)
            y_ref[...] = y * lax.rsqrt(ms + EPS) * nf_ref[...]


def _ffn(h, norm2_w, w_up, w_down, final_w, *, final_norm, emit_weights, bm, bf):
    m, d = h.shape
    dff = w_up.shape[1]
    wbytes = w_up.dtype.itemsize
    uspec = pl.BlockSpec((d, bf), lambda i, f: (0, f))
    dspec = pl.BlockSpec((bf, d), lambda i, f: (f, 0))
    yspec = pl.BlockSpec((bm, d), lambda i, f: (i, 0))
    yshape = jax.ShapeDtypeStruct((m, d), F32)
    vmem = (bm * d * 4 + 2 * bm * d * 4 + bm * d * 2 + 2 * 2 * d * bf * wbytes + bm * bf * 8 + bm * d * 4
            + 2 * d * bf * 2 + (2 << 20))
    if emit_weights:
        assert m == bm, "every weight tile must be visited exactly once"
        out_specs = [yspec, uspec, dspec]
        out_shape = [yshape, jax.ShapeDtypeStruct(w_up.shape, BF16), jax.ShapeDtypeStruct(w_down.shape, BF16)]
        vmem += 2 * 2 * d * bf * 2
    else:
        out_specs = yspec
        out_shape = yshape
    return pl.pallas_call(
        functools.partial(_ffn_kernel, final_norm=final_norm, emit_weights=emit_weights),
        grid=(m // bm, dff // bf),
        in_specs=[pl.BlockSpec((bm, d), lambda i, f: (i, 0), pipeline_mode=pl.Buffered(1)),
                  pl.BlockSpec((1, d), lambda i, f: (0, 0)),
                  uspec, dspec,
                  pl.BlockSpec((1, d), lambda i, f: (0, 0))],
        out_specs=out_specs,
        out_shape=out_shape,
        scratch_shapes=[pltpu.VMEM((bm, d), BF16)],
        compiler_params=_cparams(("parallel", "arbitrary"), vmem),
        name="ffn",
    )(h, norm2_w.reshape(1, d), w_up, w_down, final_w.reshape(1, d))


def _lambda_full(lam_ref, lam_init):
    lam = lam_ref[...]
    s1 = jnp.sum(lam[0:1, :] * lam[1:2, :], axis=-1, keepdims=True)
    s2 = jnp.sum(lam[2:3, :] * lam[3:4, :], axis=-1, keepdims=True)
    return jnp.exp(s1) - jnp.exp(s2) + lam_init


def _diff_combine(acc0, l0, acc1, l1, lam, sub_w, out_scale):
    o = acc0 / l0 - lam * (acc1 / l1)
    ms = jnp.mean(o * o, axis=-1, keepdims=True)
    return o * lax.rsqrt(ms + EPS) * sub_w * out_scale


def _lane_tile(x, width):
    return jnp.concatenate([x] * (width // LANES), axis=1)


def _prompt_attn_kernel(lam_ref, sub_ref, q_ref, k_ref, v_ref, o_ref, qs_ref, s0_ref, s1_ref, p0_ref, p1_ref,
                        a0_ref, a1_ref, m_ref, l_ref, acc_ref, *, bq, dh, lam_init):
    qi = pl.program_id(1)
    q = q_ref[...]
    zeros = jnp.zeros((bq, dh), q.dtype)
    qs_ref[:bq, :] = jnp.concatenate([q[:, :dh], zeros], axis=1)
    qs_ref[bq:, :] = jnp.concatenate([zeros, q[:, dh:]], axis=1)
    m_ref[...] = jnp.full(m_ref.shape, NEG_INF, F32)
    l_ref[...] = jnp.zeros(l_ref.shape, F32)
    acc_ref[...] = jnp.zeros(acc_ref.shape, F32)

    def keys(j):
        return pl.ds(pl.multiple_of(j * bq, bq), bq)

    def scores(j, s_ref):
        s_ref[...] = _dot_nt(qs_ref[...], k_ref[keys(j), :])

    def softmax(s_ref, p_ref, a_ref, diagonal):
        s = s_ref[...]
        if diagonal:
            row = lax.broadcasted_iota(jnp.int32, s.shape, 0) % bq
            col = lax.broadcasted_iota(jnp.int32, s.shape, 1)
            s = jnp.where(row >= col, s, NEG_INF)
        m_prev = m_ref[...]
        m_new = jnp.maximum(m_prev, jnp.max(s, axis=1, keepdims=True))
        alpha = jnp.exp2(m_prev - m_new)
        p = jnp.exp2(s - _lane_tile(m_new, s.shape[1]))
        psum = p[:, :LANES]
        for t in range(1, s.shape[1] // LANES):
            psum = psum + p[:, t * LANES:(t + 1) * LANES]
        l_ref[...] = alpha * l_ref[...] + psum
        m_ref[...] = m_new
        a_ref[...] = alpha
        p_ref[...] = p.astype(p_ref.dtype)

    def mix(j, p_ref, a_ref):
        acc_ref[...] = (_lane_tile(a_ref[...], acc_ref.shape[1]) * acc_ref[...]
                        + _dot(p_ref[...], v_ref[keys(j), :]))

    scores(0, s0_ref)

    def pair(t, carry):
        j = 2 * t
        scores(j + 1, s1_ref)
        softmax(s0_ref, p0_ref, a0_ref, False)
        mix(j, p0_ref, a0_ref)
        softmax(s1_ref, p1_ref, a1_ref, False)
        scores(j + 2, s0_ref)
        mix(j + 1, p1_ref, a1_ref)
        return carry

    lax.fori_loop(0, qi // 2, pair, 0)
    j = 2 * (qi // 2)

    @pl.when(qi % 2 == 0)
    def _():
        softmax(s0_ref, p0_ref, a0_ref, True)
        mix(j, p0_ref, a0_ref)

    @pl.when(qi % 2 == 1)
    def _():
        scores(j + 1, s1_ref)
        softmax(s0_ref, p0_ref, a0_ref, False)
        mix(j, p0_ref, a0_ref)
        softmax(s1_ref, p1_ref, a1_ref, True)
        mix(j + 1, p1_ref, a1_ref)

    l = jnp.sum(l_ref[...], axis=1, keepdims=True)
    acc = acc_ref[...]
    lam = _lambda_full(lam_ref, lam_init)
    o = _diff_combine(acc[:bq], l[:bq], acc[bq:], l[bq:], lam, sub_ref[...], 1.0 - lam_init)
    o_ref[...] = o.astype(o_ref.dtype)


def _prompt_attn(q, kb, vb, lam_rows, subln_w, *, n_heads, dh, lam_init, bq):
    m, width = q.shape
    hw = 2 * dh
    rows = 2 * bq
    assert width == n_heads * hw and bq % LANES == 0
    scratch = [pltpu.VMEM((rows, hw), BF16),
               pltpu.VMEM((rows, bq), F32), pltpu.VMEM((rows, bq), F32),
               pltpu.VMEM((rows, bq), BF16), pltpu.VMEM((rows, bq), BF16),
               pltpu.VMEM((rows, LANES), F32), pltpu.VMEM((rows, LANES), F32),
               pltpu.VMEM((rows, LANES), F32), pltpu.VMEM((rows, LANES), F32),
               pltpu.VMEM((rows, hw), F32)]
    scratch_bytes = rows * (hw * 2 + 2 * bq * 4 + 2 * bq * 2 + 4 * LANES * 4 + hw * 4)
    vmem = 2 * (2 * m * hw * 2 + 2 * bq * hw * 2) + scratch_bytes + 3 * rows * bq * 4 + (2 << 20)
    return pl.pallas_call(
        functools.partial(_prompt_attn_kernel, bq=bq, dh=dh, lam_init=lam_init),
        grid=(n_heads, m // bq),
        in_specs=[pl.BlockSpec((4, dh), lambda h, i: (0, 0)),
                  pl.BlockSpec((1, hw), lambda h, i: (0, 0)),
                  pl.BlockSpec((bq, hw), lambda h, i: (i, h)),
                  pl.BlockSpec((m, hw), lambda h, i: (0, h)),
                  pl.BlockSpec((m, hw), lambda h, i: (0, h))],
        out_specs=pl.BlockSpec((bq, hw), lambda h, i: (i, h)),
        out_shape=jax.ShapeDtypeStruct((m, width), BF16),
        scratch_shapes=scratch,
        compiler_params=_cparams(("parallel", "arbitrary"), vmem),
        name="prompt_attn",
    )(lam_rows, subln_w.reshape(1, hw), q, kb, vb)


def _split3_bf16(x):
    hi = x.astype(BF16)
    r1 = x - hi.astype(F32)
    mid = r1.astype(BF16)
    lo = (r1 - mid.astype(F32)).astype(BF16)
    return hi, mid, lo


def _group_row(x, group, row):
    n, lanes = x.shape
    x3 = x.reshape(n // group, group, lanes)
    return jnp.broadcast_to(x3[:, row:row + 1, :], x3.shape).reshape(n, lanes)


def _hgrn_chunk(qr, g, kk, v, st, tril, dsum):
    c, kd = g.shape
    vb = v.astype(BF16)
    g3 = jnp.concatenate(_split3_bf16(g), axis=1)
    b3 = _dot(tril, g3)
    b = b3[:, :kd] + b3[:, kd:2 * kd] + b3[:, 2 * kd:]
    b_last = b[c - 1:c, :]
    row = lax.broadcasted_iota(jnp.int32, (c, kd), 0)
    trow = lax.broadcasted_iota(jnp.int32, (c, c), 0)
    tcol = lax.broadcasted_iota(jnp.int32, (c, c), 1)

    o = _dot_nt((qr * jnp.exp2(b)).astype(BF16), st.astype(BF16))
    k_state = (kk * jnp.exp2(b_last - b)).astype(BF16)
    st_new = jnp.exp2(b_last) * st + _dot_tn(vb, k_state)

    b_k = b - jnp.log2(kk)
    parts = [(qr * jnp.exp2(jnp.minimum(b - _group_row(b_k, SUBLANES, j), 0.0))).astype(BF16)
             for j in range(SUBLANES)]
    in_group = (trow // SUBLANES == tcol // SUBLANES) & (tcol % SUBLANES <= trow % SUBLANES)
    a = jnp.where(in_group, _dot(jnp.concatenate(parts, axis=1), dsum), 0.0)

    m = SUBLANES
    while m < c:
        e = jnp.exp2(-jnp.abs(b - _group_row(b, 2 * m, m - 1)))
        right = (row % (2 * m)) >= m
        qm = jnp.where(right, qr * e, 0.0).astype(BF16)
        km = jnp.where(right, 0.0, kk * e).astype(BF16)
        a = a + jnp.where(trow // (2 * m) == tcol // (2 * m), _dot_nt(qm, km), 0.0)
        m *= 2

    o = o + _dot(a.astype(BF16), vb)
    return o, st_new


def _prompt_hgrn_kernel(q_ref, g_ref, k_ref, v_ref, t_ref, nw_ref, tril_ref, dsum_ref, o_ref, s_ref, st_ref, *,
                        chunk):
    ci = pl.program_id(1)

    @pl.when(ci == 0)
    def _():
        st_ref[...] = jnp.zeros(st_ref.shape, F32)

    tril = tril_ref[...]
    dsum = dsum_ref[...]
    hb, bt, kd = q_ref.shape
    st = [st_ref[h] for h in range(hb)]
    for sub in range(bt // chunk):
        rows = slice(sub * chunk, (sub + 1) * chunk)
        for h in range(hb):
            o, st[h] = _hgrn_chunk(q_ref[h, rows, :], g_ref[h, rows, :], k_ref[h, rows, :], v_ref[h, rows, :],
                                   st[h], tril, dsum)
            ms = jnp.mean(o * o, axis=-1, keepdims=True)
            o_ref[rows, h * kd:(h + 1) * kd] = (o * lax.rsqrt(ms + EPS) * nw_ref[...]
                                                * t_ref[h, rows, :]).astype(o_ref.dtype)
    for h in range(hb):
        st_ref[h] = st[h]

    @pl.when(ci == pl.num_programs(1) - 1)
    def _():
        for h in range(hb):
            s_ref[h] = st[h].T


def _prompt_hgrn(qr, g, kk, vr, gate, rec_norm_w, *, chunk, bt, heads_per_step):
    nh, m, kd = qr.shape
    hb = heads_per_step
    assert vr.shape[2] == kd and nh % hb == 0
    tril = jnp.tril(jnp.ones((chunk, chunk), F32)).astype(BF16)
    dsum = (jnp.arange(SUBLANES * kd)[:, None] // kd == jnp.arange(chunk)[None, :] % SUBLANES).astype(BF16)
    ispec = pl.BlockSpec((hb, bt, kd), lambda h, c: (h, c, 0))
    vmem = hb * (2 * (5 * bt * kd * 4 + bt * kd * 2) + 64 * chunk * kd * 4) + (4 << 20)
    return pl.pallas_call(
        functools.partial(_prompt_hgrn_kernel, chunk=chunk),
        grid=(nh // hb, m // bt),
        in_specs=[ispec] * 5 + [pl.BlockSpec((1, kd), lambda h, c: (0, 0)),
                                pl.BlockSpec((chunk, chunk), lambda h, c: (0, 0)),
                                pl.BlockSpec((SUBLANES * kd, chunk), lambda h, c: (0, 0))],
        out_specs=[pl.BlockSpec((bt, hb * kd), lambda h, c: (c, h)),
                   pl.BlockSpec((hb, kd, kd), lambda h, c: (h, 0, 0))],
        out_shape=[jax.ShapeDtypeStruct((m, nh * kd), BF16),
                   jax.ShapeDtypeStruct((nh, kd, kd), F32)],
        scratch_shapes=[pltpu.VMEM((hb, kd, kd), F32)],
        compiler_params=_cparams(("parallel", "arbitrary"), vmem),
        name="prompt_hgrn",
    )(qr, g, kk, vr, gate, rec_norm_w.reshape(1, kd), tril, dsum)


def _swap_halves(x, dh):
    return jnp.concatenate([x[..., dh:], x[..., :dh]], axis=-1)


def _sample_attn_kernel(pt_ref, lam_ref, sub_ref, spread_ref, q_ref, ks_ref, vs_ref, *rest, n_pages_step, dh,
                        lam_init):
    del pt_ref
    g = n_pages_step
    k_refs = rest[:g]
    v_refs = rest[g:2 * g]
    o_ref, st_ref, m_ref, l_ref, same_ref, cross_ref = rest[2 * g:]
    step = pl.program_id(1)
    n_steps = pl.num_programs(1) // 2
    n_heads, hw = q_ref.shape
    page = k_refs[0].shape[0]
    cols = page * n_heads
    q = q_ref[...].astype(F32)
    lane = lax.broadcasted_iota(jnp.int32, (n_heads, hw), 1)
    q_rows = jnp.concatenate([jnp.where(lane < dh, q, 0.0), jnp.where(lane < dh, 0.0, q)], axis=0).astype(BF16)
    own_head = (lax.broadcasted_iota(jnp.int32, (2 * n_heads, cols), 1) % n_heads
                == lax.broadcasted_iota(jnp.int32, (2 * n_heads, cols), 0) % n_heads)
    x_self = ks_ref[...] * q
    s_self = [jnp.sum(x_self[:, c * dh:(c + 1) * dh], axis=1, keepdims=True) for c in range(2)]

    def spread(col):
        sel = (lax.broadcasted_iota(jnp.int32, (2 * n_heads, LANES), 1)
               == lax.broadcasted_iota(jnp.int32, (2 * n_heads, LANES), 0) % n_heads)
        terms = _split3_bf16(jnp.where(sel, col, 0.0))
        wide = _dot_tn(jnp.concatenate(terms, axis=0), jnp.concatenate([spread_ref[...]] * 3, axis=0))
        return wide[:n_heads, :]

    @pl.when(step == 0)
    def _():
        m_ref[...] = jnp.full(m_ref.shape, NEG_INF, F32)
        l_ref[...] = jnp.zeros(l_ref.shape, F32)
        same_ref[...] = jnp.zeros(same_ref.shape, F32)
        cross_ref[...] = jnp.zeros(cross_ref.shape, F32)

    @pl.when(step < n_steps)
    def _():
        m = m_ref[...]
        for i in range(g):
            keys = k_refs[i][...].reshape(cols, hw).astype(BF16)
            s = jnp.where(own_head, _dot_nt(q_rows, keys), NEG_INF)
            st_ref[step * g + i] = s
            m = jnp.maximum(m, jnp.max(s, axis=1, keepdims=True))
        m_ref[...] = m

    @pl.when(step == n_steps - 1)
    def _():
        m_ref[...] = jnp.maximum(m_ref[...], jnp.concatenate(s_self, axis=0))

    @pl.when(step >= n_steps)
    def _():
        m = m_ref[...]
        l = l_ref[...]
        same = same_ref[...]
        cross = cross_ref[...]
        for i in range(g):
            p = jnp.exp(st_ref[(step - n_steps) * g + i] - m)
            l = l + jnp.sum(p, axis=1, keepdims=True)
            p_page = _dot_tn(p.astype(BF16), spread_ref[...]).reshape(page, n_heads, hw)
            v = v_refs[i][...]
            same = same + jnp.sum(p_page * v, axis=0)
            cross = cross + jnp.sum(p_page * _swap_halves(v, dh), axis=0)
        l_ref[...] = l
        same_ref[...] = same
        cross_ref[...] = cross

    @pl.when(step == 2 * n_steps - 1)
    def _():
        s_wide = jnp.concatenate([jnp.broadcast_to(s, (n_heads, dh)) for s in s_self], axis=1)
        p_self = jnp.exp(s_wide - spread(m_ref[...]))
        v_self = vs_ref[...]
        l = spread(l_ref[...]) + p_self
        same = same_ref[...] + p_self * v_self
        cross = cross_ref[...] + p_self * _swap_halves(v_self, dh)
        acc0 = jnp.concatenate([same[:, :dh], cross[:, :dh]], axis=-1)
        acc1 = jnp.concatenate([cross[:, dh:], same[:, dh:]], axis=-1)
        l0 = jnp.concatenate([l[:, :dh], l[:, :dh]], axis=-1)
        l1 = jnp.concatenate([l[:, dh:], l[:, dh:]], axis=-1)
        lam = _lambda_full(lam_ref, lam_init)
        o = _diff_combine(acc0, l0, acc1, l1, lam, sub_ref[...], 1.0 - lam_init)
        o_ref[...] = o.astype(o_ref.dtype)


def _sample_attn(q, k_self, v_self, pool_k, pool_v, page_ids, lam_rows, subln_w, *, n_heads, dh, lam_init,
                 pages_per_step):
    nb, width = q.shape
    n_pages = page_ids.shape[1]
    page = pool_k.shape[1]
    hw = 2 * dh
    g = pages_per_step
    assert n_pages % g == 0 and pool_k.shape[2:] == (n_heads, hw) and width == n_heads * hw
    spread = (jnp.arange(2 * n_heads)[:, None] // n_heads == jnp.arange(hw)[None, :] // dh).astype(BF16)
    steps = n_pages // g

    def key_spec(i):
        return pl.BlockSpec((None, page, n_heads, hw),
                            lambda b, p, pt: (pt[b, jnp.minimum(p, steps - 1) * g + i], 0, 0, 0))

    def value_spec(i):
        return pl.BlockSpec((None, page, n_heads, hw),
                            lambda b, p, pt: (pt[b, jnp.maximum(p - steps, 0) * g + i], 0, 0, 0))

    row_spec = pl.BlockSpec((None, n_heads, hw), lambda b, p, pt: (b, 0, 0))
    in_specs = ([pl.BlockSpec((4, dh), lambda b, p, pt: (0, 0)),
                 pl.BlockSpec((1, hw), lambda b, p, pt: (0, 0)),
                 pl.BlockSpec((2 * n_heads, hw), lambda b, p, pt: (0, 0)),
                 row_spec, row_spec, row_spec]
                + [key_spec(i) for i in range(g)] + [value_spec(i) for i in range(g)])
    page_bytes = page * width * 4
    score_bytes = n_pages * 2 * n_heads * page * n_heads * 4
    vmem = 2 * 2 * g * page_bytes + score_bytes + 6 * page_bytes + (4 << 20)
    scratch = [pltpu.VMEM((n_pages, 2 * n_heads, page * n_heads), F32),
               pltpu.VMEM((2 * n_heads, 1), F32), pltpu.VMEM((2 * n_heads, 1), F32),
               pltpu.VMEM((n_heads, hw), F32), pltpu.VMEM((n_heads, hw), F32)]
    out = pl.pallas_call(
        functools.partial(_sample_attn_kernel, n_pages_step=g, dh=dh, lam_init=lam_init),
        grid_spec=pltpu.PrefetchScalarGridSpec(
            num_scalar_prefetch=1,
            grid=(nb, 2 * steps),
            in_specs=in_specs,
            out_specs=row_spec,
            scratch_shapes=scratch),
        out_shape=jax.ShapeDtypeStruct((nb, n_heads, hw), BF16),
        compiler_params=_cparams(("parallel", "arbitrary"), vmem),
        name="sample_attn",
    )(page_ids, lam_rows, subln_w.reshape(1, hw), spread, q.reshape(nb, n_heads, hw),
      k_self.reshape(nb, n_heads, hw), v_self.reshape(nb, n_heads, hw), *([pool_k] * g), *([pool_v] * g))
    return out.reshape(nb, width)


def _hi_lo(x):
    hi = x.astype(BF16).astype(F32)
    return hi, x - hi


def _sample_hgrn_kernel(q_ref, g_ref, k_ref, v_ref, t_ref, nw_ref, s_ref, o_ref, so_ref, orow_ref):
    nh, nb, kd = q_ref.shape
    f_t = jnp.exp2(g_ref[...]).reshape(nh * nb, kd).T
    own = (lax.broadcasted_iota(jnp.int32, (nb, nb * kd), 1) // kd
           == lax.broadcasted_iota(jnp.int32, (nb, nb * kd), 0))

    def own_block(x):
        return jnp.where(own, jnp.concatenate([x] * nb, axis=1), 0.0)

    for h in range(nh):
        k_hi, k_lo = _hi_lo(k_ref[h])
        v_hi, v_lo = _hi_lo(v_ref[h])
        ka = jnp.concatenate([k_hi, k_hi, k_lo, jnp.zeros_like(k_lo)], axis=0).astype(BF16)
        vb = jnp.concatenate([own_block(v_hi), own_block(v_lo), own_block(v_hi),
                              jnp.zeros((nb, nb * kd), F32)], axis=0).astype(BF16)
        kv = _dot_tn(ka, vb)
        s_new = []
        for i in range(nb):
            r = h * nb + i
            s_i = f_t[:, r:r + 1] * s_ref[i, h] + kv[:, i * kd:(i + 1) * kd]
            so_ref[i, h] = s_i
            s_new.append(s_i.astype(BF16))
        q_rows = jnp.concatenate([q_ref[h], jnp.zeros((nb, kd), F32)], axis=0).astype(BF16)
        o_all = _dot(q_rows, jnp.concatenate(s_new, axis=1))
        o = jnp.concatenate([o_all[i:i + 1, i * kd:(i + 1) * kd] for i in range(nb)], axis=0)
        ms = jnp.mean(o * o, axis=-1, keepdims=True)
        orow_ref[:, h * kd:(h + 1) * kd] = o * lax.rsqrt(ms + EPS) * nw_ref[...] * t_ref[h]
    o_ref[...] = orow_ref[...].astype(o_ref.dtype)


def _sample_hgrn(qr, g, kk, vr, gate, rec_norm_w, state, *, bb):
    nh, nb, kd = qr.shape
    vd = vr.shape[2]
    assert state.shape == (nb, nh, kd, vd) and kd == vd
    ispec = pl.BlockSpec((nh, bb, kd), lambda b: (0, b, 0))
    sspec = pl.BlockSpec((bb, nh, kd, vd), lambda b: (b, 0, 0, 0))
    vmem = 2 * 2 * bb * nh * kd * vd * 4 + 2 * 5 * nh * bb * kd * 4 + 8 * nh * bb * kd * 4 + (4 << 20)
    return pl.pallas_call(
        _sample_hgrn_kernel,
        grid=(nb // bb,),
        in_specs=[ispec] * 5 + [pl.BlockSpec((1, vd), lambda b: (0, 0)), sspec],
        out_specs=[pl.BlockSpec((bb, nh * vd), lambda b: (b, 0)), sspec],
        out_shape=[jax.ShapeDtypeStruct((nb, nh * vd), BF16),
                   jax.ShapeDtypeStruct(state.shape, F32)],
        scratch_shapes=[pltpu.VMEM((bb, nh * vd), F32)],
        compiler_params=_cparams(("parallel",), vmem),
        name="sample_hgrn",
    )(qr, g, kk, vr, gate, rec_norm_w.reshape(1, vd), state)


PROJ_ROWS = 1024
PROJ_COLS = 256
PROMPT_Q_BLOCK = 512
HGRN_CHUNK = 128
HGRN_STEP = 512
HGRN_HEADS = 4
FFN_ROWS = 512
FFN_COLS = 512
SAMPLE_FFN_COLS = 256
OUT_COLS = 1024
SAMPLE_PAGES_PER_STEP = 8
SAMPLE_HGRN_SEQS = 8


def _mixer_inputs(x, l, dims, norm1_w, att_weights, rec_weights, lb_logits, *, q_scale, emit_weights):
    m = x.shape[0]
    xn = _rmsnorm(x, norm1_w[l], bm=min(m, 256))
    att, att_b = _proj_att(xn, att_weights, col0=0, att_width=dims["att_width"], scale=q_scale,
                           emit_weights=emit_weights, bm=min(m, PROJ_ROWS), bn=PROJ_COLS)
    rec, rec_b = _proj_rec(xn, rec_weights, lb_logits, layer=l, col0=3 * dims["att_width"],
                           rec_width=dims["rec_width"], hd=dims["kd"], emit_weights=emit_weights,
                           bm=min(m, PROJ_ROWS), bn=PROJ_COLS)
    return att, rec, att_b, rec_b


def kernel(x_prompt, x_sample, cache_k, cache_v, state_hgrn, page_table, norm1_w, w_in, lambda_q1, lambda_k1,
           lambda_q2, lambda_k2, subln_w, lb_logits, rec_norm_w, w_out, norm2_w, w_up, w_down, final_norm_w):
    batch, seq, d_model = x_prompt.shape
    dec_batch, dec_seq, _ = x_sample.shape
    depth, n_phys, page, n_att_heads, hw = cache_k.shape
    _, _, n_rec_heads, kd, vd = state_hgrn.shape
    assert batch == 1 and dec_seq == 1 and kd == vd
    dh = hw // 2
    dims = dict(att_width=n_att_heads * hw, rec_width=n_rec_heads * kd, dh=dh, kd=kd)
    aw = dims["att_width"]

    w_out_b = w_out.astype(BF16)
    pool_k = cache_k.reshape(depth * n_phys, page, n_att_heads, hw)
    pool_v = cache_v.reshape(depth * n_phys, page, n_att_heads, hw)

    hp = x_prompt.reshape(seq, d_model)
    hs = x_sample.reshape(dec_batch, d_model)
    outs = [[] for _ in range(6)]
    for l in range(depth):
        lam_init = 0.8 - 0.6 * math.exp(-0.3 * l)
        lam_rows = jnp.stack([lambda_q1[l], lambda_k1[l], lambda_q2[l], lambda_k2[l]])
        last = l == depth - 1

        att_scale = dh ** -0.5
        (q, k, _, v, _), (qr, g, kk, vr, gate), w_att_b, w_rec_b = _mixer_inputs(
            hs, l, dims, norm1_w, w_in[l], w_in[l], lb_logits, q_scale=att_scale, emit_weights=True)
        o_att = _sample_attn(q, k, v, pool_k, pool_v, page_table + l * n_phys, lam_rows, subln_w[l],
                             n_heads=n_att_heads, dh=dh, lam_init=lam_init, pages_per_step=SAMPLE_PAGES_PER_STEP)
        o_rec, s_s = _sample_hgrn(qr, g, kk, vr, gate, rec_norm_w[l], state_hgrn[l], bb=SAMPLE_HGRN_SEQS)
        h = _out_proj(hs, o_att, o_rec, w_out_b[l], bm=dec_batch, bn=OUT_COLS)
        hs, w_up_b, w_down_b = _ffn(h, norm2_w[l], w_up[l], w_down[l], final_norm_w, final_norm=last,
                                    emit_weights=True, bm=dec_batch, bf=SAMPLE_FFN_COLS)
        outs[3].append(k.reshape(dec_batch, dec_seq, n_att_heads, hw))
        outs[4].append(v.reshape(dec_batch, dec_seq, n_att_heads, hw))
        outs[5].append(s_s)

        (q, k, kb, v, vb), (qr, g, kk, vr, gate), _, _ = _mixer_inputs(
            hp, l, dims, norm1_w, w_att_b, w_rec_b, lb_logits, q_scale=att_scale * math.log2(math.e),
            emit_weights=False)
        o_att = _prompt_attn(q, kb, vb, lam_rows, subln_w[l], n_heads=n_att_heads, dh=dh, lam_init=lam_init,
                             bq=PROMPT_Q_BLOCK)
        o_rec, s_p = _prompt_hgrn(qr, g, kk, vr, gate, rec_norm_w[l], chunk=HGRN_CHUNK, bt=HGRN_STEP,
                                  heads_per_step=HGRN_HEADS)
        h = _out_proj(hp, o_att, o_rec, w_out_b[l], bm=FFN_ROWS, bn=OUT_COLS)
        hp = _ffn(h, norm2_w[l], w_up_b, w_down_b, final_norm_w, final_norm=last, emit_weights=False,
                  bm=FFN_ROWS, bf=FFN_COLS)
        outs[0].append(k.reshape(batch, seq, n_att_heads, hw))
        outs[1].append(v.reshape(batch, seq, n_att_heads, hw))
        outs[2].append(s_p.reshape(batch, n_rec_heads, kd, vd))

    return (hp.reshape(batch, seq, d_model), hs.reshape(dec_batch, dec_seq, d_model),
            jnp.stack(outs[0]), jnp.stack(outs[1]), jnp.stack(outs[2]),
            jnp.stack(outs[3]), jnp.stack(outs[4]), jnp.stack(outs[5]))
```

```python
import functools
import math

import jax
import jax.numpy as jnp
from jax import lax
from jax.experimental import pallas as pl
from jax.experimental.pallas import tpu as pltpu

F32 = jnp.float32
BF16 = jnp.bfloat16
EPS = 1e-6
LANES = 128
SUBLANES = 8
VMEM_CAP_BYTES = 60000 * 1024
NEG_INF = float("-inf")


def _cparams(semantics, vmem_bytes):
    return pltpu.CompilerParams(dimension_semantics=semantics,
                                vmem_limit_bytes=int(min(VMEM_CAP_BYTES, vmem_bytes)))


def _dot(a, b):
    return jnp.dot(a, b, preferred_element_type=F32)


def _dot_nt(a, b):
    return lax.dot_general(a, b, (((1,), (1,)), ((), ())), preferred_element_type=F32)


def _dot_tn(a, b):
    return lax.dot_general(a, b, (((0,), (0,)), ((), ())), preferred_element_type=F32)


def _sigmoid(x):
    return 1.0 / (1.0 + jnp.exp(-x))


def _silu(x):
    return x * _sigmoid(x)


def _rmsnorm_kernel(x_ref, w_ref, o_ref):
    x = x_ref[...]
    ms = jnp.mean(x * x, axis=-1, keepdims=True)
    o_ref[...] = (x * lax.rsqrt(ms + EPS) * w_ref[...]).astype(o_ref.dtype)


def _rmsnorm(x, w, *, bm):
    m, d = x.shape
    return pl.pallas_call(
        _rmsnorm_kernel,
        grid=(m // bm,),
        in_specs=[pl.BlockSpec((bm, d), lambda i: (i, 0)),
                  pl.BlockSpec((1, d), lambda i: (0, 0))],
        out_specs=pl.BlockSpec((bm, d), lambda i: (i, 0)),
        out_shape=jax.ShapeDtypeStruct((m, d), BF16),
        compiler_params=_cparams(("parallel",), 2 * bm * d * 6 + 2 * bm * d * 4 + (2 << 20)),
        name="rmsnorm",
    )(x, w.reshape(1, d))


def _proj_weight_specs(weights, d, bn, col0, width, n_groups):
    if isinstance(weights, (list, tuple)):
        assert len(weights) == n_groups
        return list(weights), [pl.BlockSpec((d, bn), lambda i, j: (0, j))] * n_groups
    specs = [pl.BlockSpec((d, bn), lambda i, j, g=g: (0, (col0 + g * width) // bn + j)) for g in range(n_groups)]
    return [weights] * n_groups, specs


def _load_weights(w_refs, wb_refs):
    ws = [w_ref[...].astype(BF16) for w_ref in w_refs]
    for w, wb_ref in zip(ws, wb_refs):
        wb_ref[...] = w
    return ws


def _proj_att_kernel(x_ref, *refs, scale, emit_weights):
    w_refs = refs[:3]
    q_ref, k_ref, kb_ref, v_ref, vb_ref = refs[3:8]
    wq, wk, wv = _load_weights(w_refs, refs[8:] if emit_weights else ())
    x = x_ref[...]
    q_ref[...] = (_dot(x, wq) * scale).astype(q_ref.dtype)
    k = _dot(x, wk)
    k_ref[...] = k
    kb_ref[...] = k.astype(kb_ref.dtype)
    v = _dot(x, wv)
    v_ref[...] = v
    vb_ref[...] = v.astype(vb_ref.dtype)


def _proj_att(xn, weights, *, col0, att_width, scale, emit_weights, bm, bn):
    m, d = xn.shape
    nb = att_width // bn
    assert not emit_weights or m == bm, "every weight tile must be visited exactly once"
    w_ops, wspecs = _proj_weight_specs(weights, d, bn, col0, att_width, 3)
    wbytes = w_ops[0].dtype.itemsize
    ospec = pl.BlockSpec((bm, bn), lambda i, j: (i, j))
    f32o = jax.ShapeDtypeStruct((m, att_width), F32)
    b16o = jax.ShapeDtypeStruct((m, att_width), BF16)
    out_specs = [ospec] * 5
    out_shape = [b16o, f32o, b16o, f32o, b16o]
    vmem = 2 * (bm * d * 2 + 3 * d * bn * wbytes + bm * bn * 14) + 6 * bm * bn * 4 + 3 * d * bn * 2 + (2 << 20)
    if emit_weights:
        out_specs += [pl.BlockSpec((d, bn), lambda i, j: (0, j))] * 3
        out_shape += [jax.ShapeDtypeStruct((d, att_width), BF16)] * 3
        vmem += 2 * 3 * d * bn * 2
    outs = pl.pallas_call(
        functools.partial(_proj_att_kernel, scale=scale, emit_weights=emit_weights),
        grid=(m // bm, nb),
        in_specs=[pl.BlockSpec((bm, d), lambda i, j: (i, 0))] + wspecs,
        out_specs=out_specs,
        out_shape=out_shape,
        compiler_params=_cparams(("parallel", "arbitrary"), vmem),
        name="proj_att",
    )(xn, *w_ops)
    return outs[:5], list(outs[5:])


def _store_heads(o_ref, x, hd):
    for h in range(o_ref.shape[0]):
        o_ref[h] = x[:, h * hd:(h + 1) * hd]


def _proj_rec_kernel(x_ref, *refs, hd, layer, emit_weights):
    w_refs = refs[:4]
    lb_ref = refs[4]
    q_ref, g_ref, k_ref, v_ref, t_ref = refs[5:10]
    wq, wf, wi, wg = _load_weights(w_refs, refs[10:] if emit_weights else ())
    x = x_ref[...]
    lbl = lb_ref[...]
    e = jnp.exp(lbl - jnp.max(lbl, axis=0, keepdims=True))
    lb = jnp.sum(e[:layer + 1, :], axis=0, keepdims=True) / jnp.sum(e, axis=0, keepdims=True)
    f = lb + (1.0 - lb) * _sigmoid(_dot(x, wf))
    _store_heads(q_ref, _silu(_dot(x, wq)), hd)
    _store_heads(g_ref, jnp.log2(f), hd)
    _store_heads(k_ref, 1.0 - f, hd)
    _store_heads(v_ref, _dot(x, wi), hd)
    _store_heads(t_ref, _silu(_dot(x, wg)), hd)


def _proj_rec(xn, weights, lb_logits, *, layer, col0, rec_width, hd, emit_weights, bm, bn):
    m, d = xn.shape
    nb = rec_width // bn
    hb = bn // hd
    assert not emit_weights or m == bm, "every weight tile must be visited exactly once"
    w_ops, wspecs = _proj_weight_specs(weights, d, bn, col0, rec_width, 4)
    wbytes = w_ops[0].dtype.itemsize
    lbspec = pl.BlockSpec((lb_logits.shape[0], bn), lambda i, j: (0, j))
    ospec = pl.BlockSpec((hb, bm, hd), lambda i, j: (j, i, 0))
    oshape = jax.ShapeDtypeStruct((rec_width // hd, m, hd), F32)
    out_specs = [ospec] * 5
    out_shape = [oshape] * 5
    vmem = 2 * (bm * d * 2 + 4 * d * bn * wbytes + 5 * bm * bn * 4) + 10 * bm * bn * 4 + 4 * d * bn * 2 + (2 << 20)
    if emit_weights:
        out_specs += [pl.BlockSpec((d, bn), lambda i, j: (0, j))] * 4
        out_shape += [jax.ShapeDtypeStruct((d, rec_width), BF16)] * 4
        vmem += 2 * 4 * d * bn * 2
    outs = pl.pallas_call(
        functools.partial(_proj_rec_kernel, hd=hd, layer=layer, emit_weights=emit_weights),
        grid=(m // bm, nb),
        in_specs=[pl.BlockSpec((bm, d), lambda i, j: (i, 0))] + wspecs + [lbspec],
        out_specs=out_specs,
        out_shape=out_shape,
        compiler_params=_cparams(("parallel", "arbitrary"), vmem),
        name="proj_rec",
    )(xn, *w_ops, lb_logits)
    return outs[:5], list(outs[5:])


def _out_proj_kernel(x_ref, a_ref, r_ref, wa_ref, wr_ref, h_ref):
    h_ref[...] = x_ref[...] + _dot(a_ref[...], wa_ref[...]) + _dot(r_ref[...], wr_ref[...])


def _out_proj(x, o_att, o_rec, w_out, *, bm, bn):
    m, d = x.shape
    ka = o_att.shape[1]
    kr = o_rec.shape[1]
    assert ka == kr
    vmem = 2 * (bm * bn * 8 + bm * (ka + kr) * 2 + (ka + kr) * bn * 2) + 2 * bm * bn * 4 + (2 << 20)
    return pl.pallas_call(
        _out_proj_kernel,
        grid=(m // bm, d // bn),
        in_specs=[pl.BlockSpec((bm, bn), lambda i, j: (i, j)),
                  pl.BlockSpec((bm, ka), lambda i, j: (i, 0)),
                  pl.BlockSpec((bm, kr), lambda i, j: (i, 0)),
                  pl.BlockSpec((ka, bn), lambda i, j: (0, j)),
                  pl.BlockSpec((kr, bn), lambda i, j: (1, j))],
        out_specs=pl.BlockSpec((bm, bn), lambda i, j: (i, j)),
        out_shape=jax.ShapeDtypeStruct((m, d), F32),
        compiler_params=_cparams(("parallel", "arbitrary"), vmem),
        name="out_proj",
    )(x, o_att, o_rec, w_out, w_out)


def _ffn_kernel(h_ref, n2_ref, wu_ref, wd_ref, nf_ref, y_ref, *rest, final_norm, emit_weights):
    if emit_weights:
        wub_ref, wdb_ref, hn_ref = rest
    else:
        (hn_ref,) = rest
    f = pl.program_id(1)

    @pl.when(f == 0)
    def _():
        h = h_ref[...]
        ms = jnp.mean(h * h, axis=-1, keepdims=True)
        hn_ref[...] = (h * lax.rsqrt(ms + EPS) * n2_ref[...]).astype(hn_ref.dtype)
        y_ref[...] = h

    wu = wu_ref[...].astype(BF16)
    wd = wd_ref[...].astype(BF16)
    if emit_weights:
        wub_ref[...] = wu
        wdb_ref[...] = wd
    u = jnp.maximum(_dot(hn_ref[...], wu), 0.0)
    y_ref[...] += _dot((u * u).astype(BF16), wd)

    if final_norm:
        @pl.when(f == pl.num_programs(1) - 1)
        def _():
            y = y_ref[...]
            ms = jnp.mean(y * y, axis=-1, keepdims=True)
            y_ref[...] = y * lax.rsqrt(ms + EPS) * nf_ref[...]


def _ffn(h, norm2_w, w_up, w_down, final_w, *, final_norm, emit_weights, bm, bf):
    m, d = h.shape
    dff = w_up.shape[1]
    wbytes = w_up.dtype.itemsize
    uspec = pl.BlockSpec((d, bf), lambda i, f: (0, f))
    dspec = pl.BlockSpec((bf, d), lambda i, f: (f, 0))
    yspec = pl.BlockSpec((bm, d), lambda i, f: (i, 0))
    yshape = jax.ShapeDtypeStruct((m, d), F32)
    vmem = (bm * d * 4 + 2 * bm * d * 4 + bm * d * 2 + 2 * 2 * d * bf * wbytes + bm * bf * 8 + bm * d * 4
            + 2 * d * bf * 2 + (2 << 20))
    if emit_weights:
        assert m == bm, "every weight tile must be visited exactly once"
        out_specs = [yspec, uspec, dspec]
        out_shape = [yshape, jax.ShapeDtypeStruct(w_up.shape, BF16), jax.ShapeDtypeStruct(w_down.shape, BF16)]
        vmem += 2 * 2 * d * bf * 2
    else:
        out_specs = yspec
        out_shape = yshape
    return pl.pallas_call(
        functools.partial(_ffn_kernel, final_norm=final_norm, emit_weights=emit_weights),
        grid=(m // bm, dff // bf),
        in_specs=[pl.BlockSpec((bm, d), lambda i, f: (i, 0), pipeline_mode=pl.Buffered(1)),
                  pl.BlockSpec((1, d), lambda i, f: (0, 0)),
                  uspec, dspec,
                  pl.BlockSpec((1, d), lambda i, f: (0, 0))],
        out_specs=out_specs,
        out_shape=out_shape,
        scratch_shapes=[pltpu.VMEM((bm, d), BF16)],
        compiler_params=_cparams(("parallel", "arbitrary"), vmem),
        name="ffn",
    )(h, norm2_w.reshape(1, d), w_up, w_down, final_w.reshape(1, d))


def _lambda_full(lam_ref, lam_init):
    lam = lam_ref[...]
    s1 = jnp.sum(lam[0:1, :] * lam[1:2, :], axis=-1, keepdims=True)
    s2 = jnp.sum(lam[2:3, :] * lam[3:4, :], axis=-1, keepdims=True)
    return jnp.exp(s1) - jnp.exp(s2) + lam_init


def _diff_combine(acc0, l0, acc1, l1, lam, sub_w, out_scale):
    o = acc0 / l0 - lam * (acc1 / l1)
    ms = jnp.mean(o * o, axis=-1, keepdims=True)
    return o * lax.rsqrt(ms + EPS) * sub_w * out_scale


def _lane_tile(x, width):
    return jnp.concatenate([x] * (width // LANES), axis=1)


def _prompt_attn_kernel(lam_ref, sub_ref, q_ref, k_ref, v_ref, o_ref, qs_ref, s0_ref, s1_ref, p0_ref, p1_ref,
                        a0_ref, a1_ref, m_ref, l_ref, acc_ref, *, bq, dh, lam_init):
    qi = pl.program_id(1)
    q = q_ref[...]
    zeros = jnp.zeros((bq, dh), q.dtype)
    qs_ref[:bq, :] = jnp.concatenate([q[:, :dh], zeros], axis=1)
    qs_ref[bq:, :] = jnp.concatenate([zeros, q[:, dh:]], axis=1)
    m_ref[...] = jnp.full(m_ref.shape, NEG_INF, F32)
    l_ref[...] = jnp.zeros(l_ref.shape, F32)
    acc_ref[...] = jnp.zeros(acc_ref.shape, F32)

    def keys(j):
        return pl.ds(pl.multiple_of(j * bq, bq), bq)

    def scores(j, s_ref):
        s_ref[...] = _dot_nt(qs_ref[...], k_ref[keys(j), :])

    def softmax(s_ref, p_ref, a_ref, diagonal):
        s = s_ref[...]
        if diagonal:
            row = lax.broadcasted_iota(jnp.int32, s.shape, 0) % bq
            col = lax.broadcasted_iota(jnp.int32, s.shape, 1)
            s = jnp.where(row >= col, s, NEG_INF)
        m_prev = m_ref[...]
        m_new = jnp.maximum(m_prev, jnp.max(s, axis=1, keepdims=True))
        alpha = jnp.exp2(m_prev - m_new)
        p = jnp.exp2(s - _lane_tile(m_new, s.shape[1]))
        psum = p[:, :LANES]
        for t in range(1, s.shape[1] // LANES):
            psum = psum + p[:, t * LANES:(t + 1) * LANES]
        l_ref[...] = alpha * l_ref[...] + psum
        m_ref[...] = m_new
        a_ref[...] = alpha
        p_ref[...] = p.astype(p_ref.dtype)

    def mix(j, p_ref, a_ref):
        acc_ref[...] = (_lane_tile(a_ref[...], acc_ref.shape[1]) * acc_ref[...]
                        + _dot(p_ref[...], v_ref[keys(j), :]))

    scores(0, s0_ref)

    def pair(t, carry):
        j = 2 * t
        scores(j + 1, s1_ref)
        softmax(s0_ref, p0_ref, a0_ref, False)
        mix(j, p0_ref, a0_ref)
        softmax(s1_ref, p1_ref, a1_ref, False)
        scores(j + 2, s0_ref)
        mix(j + 1, p1_ref, a1_ref)
        return carry

    lax.fori_loop(0, qi // 2, pair, 0)
    j = 2 * (qi // 2)

    @pl.when(qi % 2 == 0)
    def _():
        softmax(s0_ref, p0_ref, a0_ref, True)
        mix(j, p0_ref, a0_ref)

    @pl.when(qi % 2 == 1)
    def _():
        scores(j + 1, s1_ref)
        softmax(s0_ref, p0_ref, a0_ref, False)
        mix(j, p0_ref, a0_ref)
        softmax(s1_ref, p1_ref, a1_ref, True)
        mix(j + 1, p1_ref, a1_ref)

    l = jnp.sum(l_ref[...], axis=1, keepdims=True)
    acc = acc_ref[...]
    lam = _lambda_full(lam_ref, lam_init)
    o = _diff_combine(acc[:bq], l[:bq], acc[bq:], l[bq:], lam, sub_ref[...], 1.0 - lam_init)
    o_ref[...] = o.astype(o_ref.dtype)


def _prompt_attn(q, kb, vb, lam_rows, subln_w, *, n_heads, dh, lam_init, bq):
    m, width = q.shape
    hw = 2 * dh
    rows = 2 * bq
    assert width == n_heads * hw and bq % LANES == 0
    scratch = [pltpu.VMEM((rows, hw), BF16),
               pltpu.VMEM((rows, bq), F32), pltpu.VMEM((rows, bq), F32),
               pltpu.VMEM((rows, bq), BF16), pltpu.VMEM((rows, bq), BF16),
               pltpu.VMEM((rows, LANES), F32), pltpu.VMEM((rows, LANES), F32),
               pltpu.VMEM((rows, LANES), F32), pltpu.VMEM((rows, LANES), F32),
               pltpu.VMEM((rows, hw), F32)]
    scratch_bytes = rows * (hw * 2 + 2 * bq * 4 + 2 * bq * 2 + 4 * LANES * 4 + hw * 4)
    vmem = 2 * (2 * m * hw * 2 + 2 * bq * hw * 2) + scratch_bytes + 3 * rows * bq * 4 + (2 << 20)
    return pl.pallas_call(
        functools.partial(_prompt_attn_kernel, bq=bq, dh=dh, lam_init=lam_init),
        grid=(n_heads, m // bq),
        in_specs=[pl.BlockSpec((4, dh), lambda h, i: (0, 0)),
                  pl.BlockSpec((1, hw), lambda h, i: (0, 0)),
                  pl.BlockSpec((bq, hw), lambda h, i: (i, h)),
                  pl.BlockSpec((m, hw), lambda h, i: (0, h)),
                  pl.BlockSpec((m, hw), lambda h, i: (0, h))],
        out_specs=pl.BlockSpec((bq, hw), lambda h, i: (i, h)),
        out_shape=jax.ShapeDtypeStruct((m, width), BF16),
        scratch_shapes=scratch,
        compiler_params=_cparams(("parallel", "arbitrary"), vmem),
        name="prompt_attn",
    )(lam_rows, subln_w.reshape(1, hw), q, kb, vb)


def _split3_bf16(x):
    hi = x.astype(BF16)
    r1 = x - hi.astype(F32)
    mid = r1.astype(BF16)
    lo = (r1 - mid.astype(F32)).astype(BF16)
    return hi, mid, lo


def _group_row(x, group, row):
    n, lanes = x.shape
    x3 = x.reshape(n // group, group, lanes)
    return jnp.broadcast_to(x3[:, row:row + 1, :], x3.shape).reshape(n, lanes)


def _hgrn_chunk(qr, g, kk, v, st, tril, dsum):
    c, kd = g.shape
    vb = v.astype(BF16)
    g3 = jnp.concatenate(_split3_bf16(g), axis=1)
    b3 = _dot(tril, g3)
    b = b3[:, :kd] + b3[:, kd:2 * kd] + b3[:, 2 * kd:]
    b_last = b[c - 1:c, :]
    row = lax.broadcasted_iota(jnp.int32, (c, kd), 0)
    trow = lax.broadcasted_iota(jnp.int32, (c, c), 0)
    tcol = lax.broadcasted_iota(jnp.int32, (c, c), 1)

    o = _dot_nt((qr * jnp.exp2(b)).astype(BF16), st.astype(BF16))
    k_state = (kk * jnp.exp2(b_last - b)).astype(BF16)
    st_new = jnp.exp2(b_last) * st + _dot_tn(vb, k_state)

    b_k = b - jnp.log2(kk)
    parts = [(qr * jnp.exp2(jnp.minimum(b - _group_row(b_k, SUBLANES, j), 0.0))).astype(BF16)
             for j in range(SUBLANES)]
    in_group = (trow // SUBLANES == tcol // SUBLANES) & (tcol % SUBLANES <= trow % SUBLANES)
    a = jnp.where(in_group, _dot(jnp.concatenate(parts, axis=1), dsum), 0.0)

    m = SUBLANES
    while m < c:
        e = jnp.exp2(-jnp.abs(b - _group_row(b, 2 * m, m - 1)))
        right = (row % (2 * m)) >= m
        qm = jnp.where(right, qr * e, 0.0).astype(BF16)
        km = jnp.where(right, 0.0, kk * e).astype(BF16)
        a = a + jnp.where(trow // (2 * m) == tcol // (2 * m), _dot_nt(qm, km), 0.0)
        m *= 2

    o = o + _dot(a.astype(BF16), vb)
    return o, st_new


def _prompt_hgrn_kernel(q_ref, g_ref, k_ref, v_ref, t_ref, nw_ref, tril_ref, dsum_ref, o_ref, s_ref, st_ref, *,
                        chunk):
    ci = pl.program_id(1)

    @pl.when(ci == 0)
    def _():
        st_ref[...] = jnp.zeros(st_ref.shape, F32)

    tril = tril_ref[...]
    dsum = dsum_ref[...]
    hb, bt, kd = q_ref.shape
    st = [st_ref[h] for h in range(hb)]
    for sub in range(bt // chunk):
        rows = slice(sub * chunk, (sub + 1) * chunk)
        for h in range(hb):
            o, st[h] = _hgrn_chunk(q_ref[h, rows, :], g_ref[h, rows, :], k_ref[h, rows, :], v_ref[h, rows, :],
                                   st[h], tril, dsum)
            ms = jnp.mean(o * o, axis=-1, keepdims=True)
            o_ref[rows, h * kd:(h + 1) * kd] = (o * lax.rsqrt(ms + EPS) * nw_ref[...]
                                                * t_ref[h, rows, :]).astype(o_ref.dtype)
    for h in range(hb):
        st_ref[h] = st[h]

    @pl.when(ci == pl.num_programs(1) - 1)
    def _():
        for h in range(hb):
            s_ref[h] = st[h].T


def _prompt_hgrn(qr, g, kk, vr, gate, rec_norm_w, *, chunk, bt, heads_per_step):
    nh, m, kd = qr.shape
    hb = heads_per_step
    assert vr.shape[2] == kd and nh % hb == 0
    tril = jnp.tril(jnp.ones((chunk, chunk), F32)).astype(BF16)
    dsum = (jnp.arange(SUBLANES * kd)[:, None] // kd == jnp.arange(chunk)[None, :] % SUBLANES).astype(BF16)
    ispec = pl.BlockSpec((hb, bt, kd), lambda h, c: (h, c, 0))
    vmem = hb * (2 * (5 * bt * kd * 4 + bt * kd * 2) + 64 * chunk * kd * 4) + (4 << 20)
    return pl.pallas_call(
        functools.partial(_prompt_hgrn_kernel, chunk=chunk),
        grid=(nh // hb, m // bt),
        in_specs=[ispec] * 5 + [pl.BlockSpec((1, kd), lambda h, c: (0, 0)),
                                pl.BlockSpec((chunk, chunk), lambda h, c: (0, 0)),
                                pl.BlockSpec((SUBLANES * kd, chunk), lambda h, c: (0, 0))],
        out_specs=[pl.BlockSpec((bt, hb * kd), lambda h, c: (c, h)),
                   pl.BlockSpec((hb, kd, kd), lambda h, c: (h, 0, 0))],
        out_shape=[jax.ShapeDtypeStruct((m, nh * kd), BF16),
                   jax.ShapeDtypeStruct((nh, kd, kd), F32)],
        scratch_shapes=[pltpu.VMEM((hb, kd, kd), F32)],
        compiler_params=_cparams(("parallel", "arbitrary"), vmem),
        name="prompt_hgrn",
    )(qr, g, kk, vr, gate, rec_norm_w.reshape(1, kd), tril, dsum)


def _swap_halves(x, dh):
    return jnp.concatenate([x[..., dh:], x[..., :dh]], axis=-1)


def _sample_attn_kernel(pt_ref, lam_ref, sub_ref, spread_ref, q_ref, ks_ref, vs_ref, pool_k_ref, pool_v_ref, o_ref,
                        st_ref, m_ref, l_ref, same_ref, cross_ref, buf_ref, sem_ref, *, dh, lam_init):
    n_slots, g, page, n_heads, hw = buf_ref.shape
    step = pl.program_id(1)
    steps_per_seq = pl.num_programs(1)
    n_steps = steps_per_seq // 2
    cols = page * n_heads

    t = pl.program_id(0) * steps_per_seq + step
    total = pl.num_programs(0) * steps_per_seq
    ahead = n_slots - 1

    def start_step(tt):
        seq = tt // steps_per_seq
        s = tt % steps_per_seq
        slot = tt % n_slots
        first = jnp.where(s < n_steps, s, s - n_steps) * g

        def issue(pool_ref):
            for i in range(g):
                pltpu.make_async_copy(pool_ref.at[pt_ref[seq, first + i]], buf_ref.at[slot, i],
                                      sem_ref.at[slot]).start()

        @pl.when(s < n_steps)
        def _():
            issue(pool_k_ref)

        @pl.when(s >= n_steps)
        def _():
            issue(pool_v_ref)

    @pl.when(t == 0)
    def _():
        for tt in range(ahead):
            start_step(jnp.int32(tt))

    @pl.when(t + ahead < total)
    def _():
        start_step(t + ahead)

    slot = t % n_slots
    for i in range(g):
        pltpu.make_async_copy(pool_k_ref.at[0], buf_ref.at[slot, i], sem_ref.at[slot]).wait()
    q = q_ref[...].astype(F32)
    lane = lax.broadcasted_iota(jnp.int32, (n_heads, hw), 1)
    q_rows = jnp.concatenate([jnp.where(lane < dh, q, 0.0), jnp.where(lane < dh, 0.0, q)], axis=0).astype(BF16)
    own_head = (lax.broadcasted_iota(jnp.int32, (2 * n_heads, cols), 1) % n_heads
                == lax.broadcasted_iota(jnp.int32, (2 * n_heads, cols), 0) % n_heads)
    x_self = ks_ref[...] * q
    s_self = [jnp.sum(x_self[:, c * dh:(c + 1) * dh], axis=1, keepdims=True) for c in range(2)]

    def spread(col):
        sel = (lax.broadcasted_iota(jnp.int32, (2 * n_heads, LANES), 1)
               == lax.broadcasted_iota(jnp.int32, (2 * n_heads, LANES), 0) % n_heads)
        terms = _split3_bf16(jnp.where(sel, col, 0.0))
        wide = _dot_tn(jnp.concatenate(terms, axis=0), jnp.concatenate([spread_ref[...]] * 3, axis=0))
        return wide[:n_heads, :]

    @pl.when(step == 0)
    def _():
        m_ref[...] = jnp.full(m_ref.shape, NEG_INF, F32)
        l_ref[...] = jnp.zeros(l_ref.shape, F32)
        same_ref[...] = jnp.zeros(same_ref.shape, F32)
        cross_ref[...] = jnp.zeros(cross_ref.shape, F32)

    @pl.when(step < n_steps)
    def _():
        m = m_ref[...]
        for i in range(g):
            keys = buf_ref[slot, i].reshape(cols, hw).astype(BF16)
            s = jnp.where(own_head, _dot_nt(q_rows, keys), NEG_INF)
            st_ref[step * g + i] = s
            m = jnp.maximum(m, jnp.max(s, axis=1, keepdims=True))
        m_ref[...] = m

    @pl.when(step == n_steps - 1)
    def _():
        m_ref[...] = jnp.maximum(m_ref[...], jnp.concatenate(s_self, axis=0))

    @pl.when(step >= n_steps)
    def _():
        m = m_ref[...]
        l = l_ref[...]
        same = same_ref[...]
        cross = cross_ref[...]
        for i in range(g):
            p = jnp.exp(st_ref[(step - n_steps) * g + i] - m)
            l = l + jnp.sum(p, axis=1, keepdims=True)
            p_page = _dot_tn(p.astype(BF16), spread_ref[...]).reshape(page, n_heads, hw)
            v = buf_ref[slot, i]
            same = same + jnp.sum(p_page * v, axis=0)
            cross = cross + jnp.sum(p_page * _swap_halves(v, dh), axis=0)
        l_ref[...] = l
        same_ref[...] = same
        cross_ref[...] = cross

    @pl.when(step == 2 * n_steps - 1)
    def _():
        s_wide = jnp.concatenate([jnp.broadcast_to(s, (n_heads, dh)) for s in s_self], axis=1)
        p_self = jnp.exp(s_wide - spread(m_ref[...]))
        v_self = vs_ref[...]
        l = spread(l_ref[...]) + p_self
        same = same_ref[...] + p_self * v_self
        cross = cross_ref[...] + p_self * _swap_halves(v_self, dh)
        acc0 = jnp.concatenate([same[:, :dh], cross[:, :dh]], axis=-1)
        acc1 = jnp.concatenate([cross[:, dh:], same[:, dh:]], axis=-1)
        l0 = jnp.concatenate([l[:, :dh], l[:, :dh]], axis=-1)
        l1 = jnp.concatenate([l[:, dh:], l[:, dh:]], axis=-1)
        lam = _lambda_full(lam_ref, lam_init)
        o = _diff_combine(acc0, l0, acc1, l1, lam, sub_ref[...], 1.0 - lam_init)
        o_ref[...] = o.astype(o_ref.dtype)


def _sample_attn(q, k_self, v_self, pool_k, pool_v, page_ids, lam_rows, subln_w, *, n_heads, dh, lam_init,
                 pages_per_step, ring_slots):
    nb, width = q.shape
    n_pages = page_ids.shape[1]
    page = pool_k.shape[1]
    hw = 2 * dh
    g = pages_per_step
    assert n_pages % g == 0 and pool_k.shape[2:] == (n_heads, hw) and width == n_heads * hw
    steps = n_pages // g
    assert 2 <= ring_slots <= nb * 2 * steps
    spread = (jnp.arange(2 * n_heads)[:, None] // n_heads == jnp.arange(hw)[None, :] // dh).astype(BF16)

    row_spec = pl.BlockSpec((None, n_heads, hw), lambda b, p, pt: (b, 0, 0))
    pool_spec = pl.BlockSpec(memory_space=pl.ANY)
    in_specs = [pl.BlockSpec((4, dh), lambda b, p, pt: (0, 0)),
                pl.BlockSpec((1, hw), lambda b, p, pt: (0, 0)),
                pl.BlockSpec((2 * n_heads, hw), lambda b, p, pt: (0, 0)),
                row_spec, row_spec, row_spec, pool_spec, pool_spec]
    page_bytes = page * width * 4
    score_bytes = n_pages * 2 * n_heads * page * n_heads * 4
    vmem = ring_slots * g * page_bytes + score_bytes + 6 * page_bytes + (4 << 20)
    scratch = [pltpu.VMEM((n_pages, 2 * n_heads, page * n_heads), F32),
               pltpu.VMEM((2 * n_heads, 1), F32), pltpu.VMEM((2 * n_heads, 1), F32),
               pltpu.VMEM((n_heads, hw), F32), pltpu.VMEM((n_heads, hw), F32),
               pltpu.VMEM((ring_slots, g, page, n_heads, hw), F32),
               pltpu.SemaphoreType.DMA((ring_slots,))]
    out = pl.pallas_call(
        functools.partial(_sample_attn_kernel, dh=dh, lam_init=lam_init),
        grid_spec=pltpu.PrefetchScalarGridSpec(
            num_scalar_prefetch=1,
            grid=(nb, 2 * steps),
            in_specs=in_specs,
            out_specs=row_spec,
            scratch_shapes=scratch),
        out_shape=jax.ShapeDtypeStruct((nb, n_heads, hw), BF16),
        compiler_params=_cparams(("arbitrary", "arbitrary"), vmem),
        name="sample_attn",
    )(page_ids, lam_rows, subln_w.reshape(1, hw), spread, q.reshape(nb, n_heads, hw),
      k_self.reshape(nb, n_heads, hw), v_self.reshape(nb, n_heads, hw), pool_k, pool_v)
    return out.reshape(nb, width)


def _hi_lo(x):
    hi = x.astype(BF16).astype(F32)
    return hi, x - hi


def _sample_hgrn_kernel(q_ref, g_ref, k_ref, v_ref, t_ref, nw_ref, s_ref, o_ref, so_ref, orow_ref):
    nh, nb, kd = q_ref.shape
    f_t = jnp.exp2(g_ref[...]).reshape(nh * nb, kd).T
    own = (lax.broadcasted_iota(jnp.int32, (nb, nb * kd), 1) // kd
           == lax.broadcasted_iota(jnp.int32, (nb, nb * kd), 0))

    def own_block(x):
        return jnp.where(own, jnp.concatenate([x] * nb, axis=1), 0.0)

    for h in range(nh):
        k_hi, k_lo = _hi_lo(k_ref[h])
        v_hi, v_lo = _hi_lo(v_ref[h])
        ka = jnp.concatenate([k_hi, k_hi, k_lo, jnp.zeros_like(k_lo)], axis=0).astype(BF16)
        vb = jnp.concatenate([own_block(v_hi), own_block(v_lo), own_block(v_hi),
                              jnp.zeros((nb, nb * kd), F32)], axis=0).astype(BF16)
        kv = _dot_tn(ka, vb)
        s_new = []
        for i in range(nb):
            r = h * nb + i
            s_i = f_t[:, r:r + 1] * s_ref[i, h] + kv[:, i * kd:(i + 1) * kd]
            so_ref[i, h] = s_i
            s_new.append(s_i.astype(BF16))
        q_rows = jnp.concatenate([q_ref[h], jnp.zeros((nb, kd), F32)], axis=0).astype(BF16)
        o_all = _dot(q_rows, jnp.concatenate(s_new, axis=1))
        o = jnp.concatenate([o_all[i:i + 1, i * kd:(i + 1) * kd] for i in range(nb)], axis=0)
        ms = jnp.mean(o * o, axis=-1, keepdims=True)
        orow_ref[:, h * kd:(h + 1) * kd] = o * lax.rsqrt(ms + EPS) * nw_ref[...] * t_ref[h]
    o_ref[...] = orow_ref[...].astype(o_ref.dtype)


def _sample_hgrn(qr, g, kk, vr, gate, rec_norm_w, state, *, bb):
    nh, nb, kd = qr.shape
    vd = vr.shape[2]
    assert state.shape == (nb, nh, kd, vd) and kd == vd
    ispec = pl.BlockSpec((nh, bb, kd), lambda b: (0, b, 0))
    sspec = pl.BlockSpec((bb, nh, kd, vd), lambda b: (b, 0, 0, 0))
    vmem = 2 * 2 * bb * nh * kd * vd * 4 + 2 * 5 * nh * bb * kd * 4 + 8 * nh * bb * kd * 4 + (4 << 20)
    return pl.pallas_call(
        _sample_hgrn_kernel,
        grid=(nb // bb,),
        in_specs=[ispec] * 5 + [pl.BlockSpec((1, vd), lambda b: (0, 0)), sspec],
        out_specs=[pl.BlockSpec((bb, nh * vd), lambda b: (b, 0)), sspec],
        out_shape=[jax.ShapeDtypeStruct((nb, nh * vd), BF16),
                   jax.ShapeDtypeStruct(state.shape, F32)],
        scratch_shapes=[pltpu.VMEM((bb, nh * vd), F32)],
        compiler_params=_cparams(("parallel",), vmem),
        name="sample_hgrn",
    )(qr, g, kk, vr, gate, rec_norm_w.reshape(1, vd), state)


PROJ_ROWS = 1024
PROJ_COLS = 256
PROMPT_Q_BLOCK = 512
HGRN_CHUNK = 128
HGRN_STEP = 512
HGRN_HEADS = 4
FFN_ROWS = 512
FFN_COLS = 512
SAMPLE_FFN_COLS = 256
OUT_COLS = 1024
SAMPLE_PAGES_PER_STEP = 8
SAMPLE_RING_SLOTS = 3
SAMPLE_HGRN_SEQS = 8


def _mixer_inputs(x, l, dims, norm1_w, att_weights, rec_weights, lb_logits, *, q_scale, emit_weights):
    m = x.shape[0]
    xn = _rmsnorm(x, norm1_w[l], bm=min(m, 256))
    att, att_b = _proj_att(xn, att_weights, col0=0, att_width=dims["att_width"], scale=q_scale,
                           emit_weights=emit_weights, bm=min(m, PROJ_ROWS), bn=PROJ_COLS)
    rec, rec_b = _proj_rec(xn, rec_weights, lb_logits, layer=l, col0=3 * dims["att_width"],
                           rec_width=dims["rec_width"], hd=dims["kd"], emit_weights=emit_weights,
                           bm=min(m, PROJ_ROWS), bn=PROJ_COLS)
    return att, rec, att_b, rec_b


def kernel(x_prompt, x_sample, cache_k, cache_v, state_hgrn, page_table, norm1_w, w_in, lambda_q1, lambda_k1,
           lambda_q2, lambda_k2, subln_w, lb_logits, rec_norm_w, w_out, norm2_w, w_up, w_down, final_norm_w):
    batch, seq, d_model = x_prompt.shape
    dec_batch, dec_seq, _ = x_sample.shape
    depth, n_phys, page, n_att_heads, hw = cache_k.shape
    _, _, n_rec_heads, kd, vd = state_hgrn.shape
    assert batch == 1 and dec_seq == 1 and kd == vd
    dh = hw // 2
    dims = dict(att_width=n_att_heads * hw, rec_width=n_rec_heads * kd, dh=dh, kd=kd)
    aw = dims["att_width"]

    w_out_b = w_out.astype(BF16)
    pool_k = cache_k.reshape(depth * n_phys, page, n_att_heads, hw)
    pool_v = cache_v.reshape(depth * n_phys, page, n_att_heads, hw)

    hp = x_prompt.reshape(seq, d_model)
    hs = x_sample.reshape(dec_batch, d_model)
    outs = [[] for _ in range(6)]
    for l in range(depth):
        lam_init = 0.8 - 0.6 * math.exp(-0.3 * l)
        lam_rows = jnp.stack([lambda_q1[l], lambda_k1[l], lambda_q2[l], lambda_k2[l]])
        last = l == depth - 1

        att_scale = dh ** -0.5
        (q, k, _, v, _), (qr, g, kk, vr, gate), w_att_b, w_rec_b = _mixer_inputs(
            hs, l, dims, norm1_w, w_in[l], w_in[l], lb_logits, q_scale=att_scale, emit_weights=True)
        o_att = _sample_attn(q, k, v, pool_k, pool_v, page_table + l * n_phys, lam_rows, subln_w[l],
                             n_heads=n_att_heads, dh=dh, lam_init=lam_init, pages_per_step=SAMPLE_PAGES_PER_STEP,
                             ring_slots=SAMPLE_RING_SLOTS)
        o_rec, s_s = _sample_hgrn(qr, g, kk, vr, gate, rec_norm_w[l], state_hgrn[l], bb=SAMPLE_HGRN_SEQS)
        h = _out_proj(hs, o_att, o_rec, w_out_b[l], bm=dec_batch, bn=OUT_COLS)
        hs, w_up_b, w_down_b = _ffn(h, norm2_w[l], w_up[l], w_down[l], final_norm_w, final_norm=last,
                                    emit_weights=True, bm=dec_batch, bf=SAMPLE_FFN_COLS)
        outs[3].append(k.reshape(dec_batch, dec_seq, n_att_heads, hw))
        outs[4].append(v.reshape(dec_batch, dec_seq, n_att_heads, hw))
        outs[5].append(s_s)

        (q, k, kb, v, vb), (qr, g, kk, vr, gate), _, _ = _mixer_inputs(
            hp, l, dims, norm1_w, w_att_b, w_rec_b, lb_logits, q_scale=att_scale * math.log2(math.e),
            emit_weights=False)
        o_att = _prompt_attn(q, kb, vb, lam_rows, subln_w[l], n_heads=n_att_heads, dh=dh, lam_init=lam_init,
                             bq=PROMPT_Q_BLOCK)
        o_rec, s_p = _prompt_hgrn(qr, g, kk, vr, gate, rec_norm_w[l], chunk=HGRN_CHUNK, bt=HGRN_STEP,
                                  heads_per_step=HGRN_HEADS)
        h = _out_proj(hp, o_att, o_rec, w_out_b[l], bm=FFN_ROWS, bn=OUT_COLS)
        hp = _ffn(h, norm2_w[l], w_up_b, w_down_b, final_norm_w, final_norm=last, emit_weights=False,
                  bm=FFN_ROWS, bf=FFN_COLS)
        outs[0].append(k.reshape(batch, seq, n_att_heads, hw))
        outs[1].append(v.reshape(batch, seq, n_att_heads, hw))
        outs[2].append(s_p.reshape(batch, n_rec_heads, kd, vd))

    return (hp.reshape(batch, seq, d_model), hs.reshape(dec_batch, dec_seq, d_model),
            jnp.stack(outs[0]), jnp.stack(outs[1]), jnp.stack(outs[2]),
            jnp.stack(outs[3]), jnp.stack(outs[4]), jnp.stack(outs[5]))
```

```python
import functools
import math

import jax
import jax.numpy as jnp
from jax import lax
from jax.experimental import pallas as pl
from jax.experimental.pallas import tpu as pltpu

F32 = jnp.float32
BF16 = jnp.bfloat16
EPS = 1e-6
LANES = 128
SUBLANES = 8
VMEM_CAP_BYTES = 60000 * 1024
NEG_INF = float("-inf")


def _cparams(semantics, vmem_bytes):
    return pltpu.CompilerParams(dimension_semantics=semantics,
                                vmem_limit_bytes=int(min(VMEM_CAP_BYTES, vmem_bytes)))


def _dot(a, b):
    return jnp.dot(a, b, preferred_element_type=F32)


def _dot_nt(a, b):
    return lax.dot_general(a, b, (((1,), (1,)), ((), ())), preferred_element_type=F32)


def _dot_tn(a, b):
    return lax.dot_general(a, b, (((0,), (0,)), ((), ())), preferred_element_type=F32)


def _sigmoid(x):
    return 1.0 / (1.0 + jnp.exp(-x))


def _silu(x):
    return x * _sigmoid(x)


def _rmsnorm_kernel(x_ref, w_ref, o_ref):
    x = x_ref[...]
    ms = jnp.mean(x * x, axis=-1, keepdims=True)
    o_ref[...] = (x * lax.rsqrt(ms + EPS) * w_ref[...]).astype(o_ref.dtype)


def _rmsnorm(x, w, *, bm):
    m, d = x.shape
    return pl.pallas_call(
        _rmsnorm_kernel,
        grid=(m // bm,),
        in_specs=[pl.BlockSpec((bm, d), lambda i: (i, 0)),
                  pl.BlockSpec((1, d), lambda i: (0, 0))],
        out_specs=pl.BlockSpec((bm, d), lambda i: (i, 0)),
        out_shape=jax.ShapeDtypeStruct((m, d), BF16),
        compiler_params=_cparams(("parallel",), 2 * bm * d * 6 + 2 * bm * d * 4 + (2 << 20)),
        name="rmsnorm",
    )(x, w.reshape(1, d))


def _proj_weight_specs(weights, d, bn, col0, width, n_groups):
    if isinstance(weights, (list, tuple)):
        assert len(weights) == n_groups
        return list(weights), [pl.BlockSpec((d, bn), lambda i, j: (0, j))] * n_groups
    specs = [pl.BlockSpec((d, bn), lambda i, j, g=g: (0, (col0 + g * width) // bn + j)) for g in range(n_groups)]
    return [weights] * n_groups, specs


def _load_weights(w_refs, wb_refs):
    ws = [w_ref[...].astype(BF16) for w_ref in w_refs]
    for w, wb_ref in zip(ws, wb_refs):
        wb_ref[...] = w
    return ws


def _proj_att_kernel(x_ref, *refs, scale, emit_weights):
    w_refs = refs[:3]
    q_ref, k_ref, kb_ref, v_ref, vb_ref = refs[3:8]
    wq, wk, wv = _load_weights(w_refs, refs[8:] if emit_weights else ())
    x = x_ref[...]
    q_ref[...] = (_dot(x, wq) * scale).astype(q_ref.dtype)
    k = _dot(x, wk)
    k_ref[...] = k
    kb_ref[...] = k.astype(kb_ref.dtype)
    v = _dot(x, wv)
    v_ref[...] = v
    vb_ref[...] = v.astype(vb_ref.dtype)


def _proj_att(xn, weights, *, col0, att_width, scale, emit_weights, bm, bn):
    m, d = xn.shape
    nb = att_width // bn
    assert not emit_weights or m == bm, "every weight tile must be visited exactly once"
    w_ops, wspecs = _proj_weight_specs(weights, d, bn, col0, att_width, 3)
    wbytes = w_ops[0].dtype.itemsize
    ospec = pl.BlockSpec((bm, bn), lambda i, j: (i, j))
    f32o = jax.ShapeDtypeStruct((m, att_width), F32)
    b16o = jax.ShapeDtypeStruct((m, att_width), BF16)
    out_specs = [ospec] * 5
    out_shape = [b16o, f32o, b16o, f32o, b16o]
    vmem = 2 * (bm * d * 2 + 3 * d * bn * wbytes + bm * bn * 14) + 6 * bm * bn * 4 + 3 * d * bn * 2 + (2 << 20)
    if emit_weights:
        out_specs += [pl.BlockSpec((d, bn), lambda i, j: (0, j))] * 3
        out_shape += [jax.ShapeDtypeStruct((d, att_width), BF16)] * 3
        vmem += 2 * 3 * d * bn * 2
    outs = pl.pallas_call(
        functools.partial(_proj_att_kernel, scale=scale, emit_weights=emit_weights),
        grid=(m // bm, nb),
        in_specs=[pl.BlockSpec((bm, d), lambda i, j: (i, 0))] + wspecs,
        out_specs=out_specs,
        out_shape=out_shape,
        compiler_params=_cparams(("parallel", "arbitrary"), vmem),
        name="proj_att",
    )(xn, *w_ops)
    return outs[:5], list(outs[5:])


def _store_heads(o_ref, x, hd):
    for h in range(o_ref.shape[0]):
        o_ref[h] = x[:, h * hd:(h + 1) * hd]


def _proj_rec_kernel(x_ref, *refs, hd, layer, emit_weights):
    w_refs = refs[:4]
    lb_ref = refs[4]
    q_ref, g_ref, k_ref, v_ref, t_ref = refs[5:10]
    wq, wf, wi, wg = _load_weights(w_refs, refs[10:] if emit_weights else ())
    x = x_ref[...]
    lbl = lb_ref[...]
    e = jnp.exp(lbl - jnp.max(lbl, axis=0, keepdims=True))
    lb = jnp.sum(e[:layer + 1, :], axis=0, keepdims=True) / jnp.sum(e, axis=0, keepdims=True)
    f = lb + (1.0 - lb) * _sigmoid(_dot(x, wf))
    _store_heads(q_ref, _silu(_dot(x, wq)), hd)
    _store_heads(g_ref, jnp.log2(f), hd)
    _store_heads(k_ref, 1.0 - f, hd)
    _store_heads(v_ref, _dot(x, wi), hd)
    _store_heads(t_ref, _silu(_dot(x, wg)), hd)


def _proj_rec(xn, weights, lb_logits, *, layer, col0, rec_width, hd, emit_weights, bm, bn):
    m, d = xn.shape
    nb = rec_width // bn
    hb = bn // hd
    assert not emit_weights or m == bm, "every weight tile must be visited exactly once"
    w_ops, wspecs = _proj_weight_specs(weights, d, bn, col0, rec_width, 4)
    wbytes = w_ops[0].dtype.itemsize
    lbspec = pl.BlockSpec((lb_logits.shape[0], bn), lambda i, j: (0, j))
    ospec = pl.BlockSpec((hb, bm, hd), lambda i, j: (j, i, 0))
    oshape = jax.ShapeDtypeStruct((rec_width // hd, m, hd), F32)
    out_specs = [ospec] * 5
    out_shape = [oshape] * 5
    vmem = 2 * (bm * d * 2 + 4 * d * bn * wbytes + 5 * bm * bn * 4) + 10 * bm * bn * 4 + 4 * d * bn * 2 + (2 << 20)
    if emit_weights:
        out_specs += [pl.BlockSpec((d, bn), lambda i, j: (0, j))] * 4
        out_shape += [jax.ShapeDtypeStruct((d, rec_width), BF16)] * 4
        vmem += 2 * 4 * d * bn * 2
    outs = pl.pallas_call(
        functools.partial(_proj_rec_kernel, hd=hd, layer=layer, emit_weights=emit_weights),
        grid=(m // bm, nb),
        in_specs=[pl.BlockSpec((bm, d), lambda i, j: (i, 0))] + wspecs + [lbspec],
        out_specs=out_specs,
        out_shape=out_shape,
        compiler_params=_cparams(("parallel", "arbitrary"), vmem),
        name="proj_rec",
    )(xn, *w_ops, lb_logits)
    return outs[:5], list(outs[5:])


def _out_proj_kernel(x_ref, a_ref, r_ref, wa_ref, wr_ref, h_ref):
    h_ref[...] = x_ref[...] + _dot(a_ref[...], wa_ref[...]) + _dot(r_ref[...], wr_ref[...])


def _out_proj(x, o_att, o_rec, w_out, *, bm, bn):
    m, d = x.shape
    ka = o_att.shape[1]
    kr = o_rec.shape[1]
    assert ka == kr
    vmem = 2 * (bm * bn * 8 + bm * (ka + kr) * 2 + (ka + kr) * bn * 2) + 2 * bm * bn * 4 + (2 << 20)
    return pl.pallas_call(
        _out_proj_kernel,
        grid=(m // bm, d // bn),
        in_specs=[pl.BlockSpec((bm, bn), lambda i, j: (i, j)),
                  pl.BlockSpec((bm, ka), lambda i, j: (i, 0)),
                  pl.BlockSpec((bm, kr), lambda i, j: (i, 0)),
                  pl.BlockSpec((ka, bn), lambda i, j: (0, j)),
                  pl.BlockSpec((kr, bn), lambda i, j: (1, j))],
        out_specs=pl.BlockSpec((bm, bn), lambda i, j: (i, j)),
        out_shape=jax.ShapeDtypeStruct((m, d), F32),
        compiler_params=_cparams(("parallel", "arbitrary"), vmem),
        name="out_proj",
    )(x, o_att, o_rec, w_out, w_out)


def _ffn_kernel(h_ref, n2_ref, wu_ref, wd_ref, nf_ref, y_ref, *rest, final_norm, emit_weights):
    if emit_weights:
        wub_ref, wdb_ref, hn_ref = rest
    else:
        (hn_ref,) = rest
    f = pl.program_id(1)

    @pl.when(f == 0)
    def _():
        h = h_ref[...]
        ms = jnp.mean(h * h, axis=-1, keepdims=True)
        hn_ref[...] = (h * lax.rsqrt(ms + EPS) * n2_ref[...]).astype(hn_ref.dtype)
        y_ref[...] = h

    wu = wu_ref[...].astype(BF16)
    wd = wd_ref[...].astype(BF16)
    if emit_weights:
        wub_ref[...] = wu
        wdb_ref[...] = wd
    u = jnp.maximum(_dot(hn_ref[...], wu), 0.0)
    y_ref[...] += _dot((u * u).astype(BF16), wd)

    if final_norm:
        @pl.when(f == pl.num_programs(1) - 1)
        def _():
            y = y_ref[...]
            ms = jnp.mean(y * y, axis=-1, keepdims=True)
            y_ref[...] = y * lax.rsqrt(ms + EPS) * nf_ref[...]


def _ffn(h, norm2_w, w_up, w_down, final_w, *, final_norm, emit_weights, bm, bf):
    m, d = h.shape
    dff = w_up.shape[1]
    wbytes = w_up.dtype.itemsize
    uspec = pl.BlockSpec((d, bf), lambda i, f: (0, f))
    dspec = pl.BlockSpec((bf, d), lambda i, f: (f, 0))
    yspec = pl.BlockSpec((bm, d), lambda i, f: (i, 0))
    yshape = jax.ShapeDtypeStruct((m, d), F32)
    vmem = (bm * d * 4 + 2 * bm * d * 4 + bm * d * 2 + 2 * 2 * d * bf * wbytes + bm * bf * 8 + bm * d * 4
            + 2 * d * bf * 2 + (2 << 20))
    if emit_weights:
        assert m == bm, "every weight tile must be visited exactly once"
        out_specs = [yspec, uspec, dspec]
        out_shape = [yshape, jax.ShapeDtypeStruct(w_up.shape, BF16), jax.ShapeDtypeStruct(w_down.shape, BF16)]
        vmem += 2 * 2 * d * bf * 2
    else:
        out_specs = yspec
        out_shape = yshape
    return pl.pallas_call(
        functools.partial(_ffn_kernel, final_norm=final_norm, emit_weights=emit_weights),
        grid=(m // bm, dff // bf),
        in_specs=[pl.BlockSpec((bm, d), lambda i, f: (i, 0), pipeline_mode=pl.Buffered(1)),
                  pl.BlockSpec((1, d), lambda i, f: (0, 0)),
                  uspec, dspec,
                  pl.BlockSpec((1, d), lambda i, f: (0, 0))],
        out_specs=out_specs,
        out_shape=out_shape,
        scratch_shapes=[pltpu.VMEM((bm, d), BF16)],
        compiler_params=_cparams(("parallel", "arbitrary"), vmem),
        name="ffn",
    )(h, norm2_w.reshape(1, d), w_up, w_down, final_w.reshape(1, d))


def _lambda_full(lam_ref, lam_init):
    lam = lam_ref[...]
    s1 = jnp.sum(lam[0:1, :] * lam[1:2, :], axis=-1, keepdims=True)
    s2 = jnp.sum(lam[2:3, :] * lam[3:4, :], axis=-1, keepdims=True)
    return jnp.exp(s1) - jnp.exp(s2) + lam_init


def _diff_combine(acc0, l0, acc1, l1, lam, sub_w, out_scale):
    o = acc0 / l0 - lam * (acc1 / l1)
    ms = jnp.mean(o * o, axis=-1, keepdims=True)
    return o * lax.rsqrt(ms + EPS) * sub_w * out_scale


def _lane_tile(x, width):
    return jnp.concatenate([x] * (width // LANES), axis=1)


def _prompt_attn_kernel(lam_ref, sub_ref, q_ref, k_ref, v_ref, o_ref, qs_ref, s0_ref, s1_ref, p0_ref, p1_ref,
                        a0_ref, a1_ref, m_ref, l_ref, acc_ref, *, bq, dh, lam_init):
    qi = pl.program_id(1)
    q = q_ref[...]
    zeros = jnp.zeros((bq, dh), q.dtype)
    qs_ref[:bq, :] = jnp.concatenate([q[:, :dh], zeros], axis=1)
    qs_ref[bq:, :] = jnp.concatenate([zeros, q[:, dh:]], axis=1)
    m_ref[...] = jnp.full(m_ref.shape, NEG_INF, F32)
    l_ref[...] = jnp.zeros(l_ref.shape, F32)
    acc_ref[...] = jnp.zeros(acc_ref.shape, F32)

    def keys(j):
        return pl.ds(pl.multiple_of(j * bq, bq), bq)

    def scores(j, s_ref):
        s_ref[...] = _dot_nt(qs_ref[...], k_ref[keys(j), :])

    def softmax(s_ref, p_ref, a_ref, diagonal):
        s = s_ref[...]
        if diagonal:
            row = lax.broadcasted_iota(jnp.int32, s.shape, 0) % bq
            col = lax.broadcasted_iota(jnp.int32, s.shape, 1)
            s = jnp.where(row >= col, s, NEG_INF)
        m_prev = m_ref[...]
        m_new = jnp.maximum(m_prev, jnp.max(s, axis=1, keepdims=True))
        alpha = jnp.exp2(m_prev - m_new)
        p = jnp.exp2(s - _lane_tile(m_new, s.shape[1]))
        psum = p[:, :LANES]
        for t in range(1, s.shape[1] // LANES):
            psum = psum + p[:, t * LANES:(t + 1) * LANES]
        l_ref[...] = alpha * l_ref[...] + psum
        m_ref[...] = m_new
        a_ref[...] = alpha
        p_ref[...] = p.astype(p_ref.dtype)

    def mix(j, p_ref, a_ref):
        acc_ref[...] = (_lane_tile(a_ref[...], acc_ref.shape[1]) * acc_ref[...]
                        + _dot(p_ref[...], v_ref[keys(j), :]))

    scores(0, s0_ref)

    def pair(t, carry):
        j = 2 * t
        scores(j + 1, s1_ref)
        softmax(s0_ref, p0_ref, a0_ref, False)
        mix(j, p0_ref, a0_ref)
        softmax(s1_ref, p1_ref, a1_ref, False)
        scores(j + 2, s0_ref)
        mix(j + 1, p1_ref, a1_ref)
        return carry

    lax.fori_loop(0, qi // 2, pair, 0)
    j = 2 * (qi // 2)

    @pl.when(qi % 2 == 0)
    def _():
        softmax(s0_ref, p0_ref, a0_ref, True)
        mix(j, p0_ref, a0_ref)

    @pl.when(qi % 2 == 1)
    def _():
        scores(j + 1, s1_ref)
        softmax(s0_ref, p0_ref, a0_ref, False)
        mix(j, p0_ref, a0_ref)
        softmax(s1_ref, p1_ref, a1_ref, True)
        mix(j + 1, p1_ref, a1_ref)

    l = jnp.sum(l_ref[...], axis=1, keepdims=True)
    acc = acc_ref[...]
    lam = _lambda_full(lam_ref, lam_init)
    o = _diff_combine(acc[:bq], l[:bq], acc[bq:], l[bq:], lam, sub_ref[...], 1.0 - lam_init)
    o_ref[...] = o.astype(o_ref.dtype)


def _prompt_attn(q, kb, vb, lam_rows, subln_w, *, n_heads, dh, lam_init, bq):
    m, width = q.shape
    hw = 2 * dh
    rows = 2 * bq
    assert width == n_heads * hw and bq % LANES == 0
    scratch = [pltpu.VMEM((rows, hw), BF16),
               pltpu.VMEM((rows, bq), F32), pltpu.VMEM((rows, bq), F32),
               pltpu.VMEM((rows, bq), BF16), pltpu.VMEM((rows, bq), BF16),
               pltpu.VMEM((rows, LANES), F32), pltpu.VMEM((rows, LANES), F32),
               pltpu.VMEM((rows, LANES), F32), pltpu.VMEM((rows, LANES), F32),
               pltpu.VMEM((rows, hw), F32)]
    scratch_bytes = rows * (hw * 2 + 2 * bq * 4 + 2 * bq * 2 + 4 * LANES * 4 + hw * 4)
    vmem = 2 * (2 * m * hw * 2 + 2 * bq * hw * 2) + scratch_bytes + 3 * rows * bq * 4 + (2 << 20)
    return pl.pallas_call(
        functools.partial(_prompt_attn_kernel, bq=bq, dh=dh, lam_init=lam_init),
        grid=(n_heads, m // bq),
        in_specs=[pl.BlockSpec((4, dh), lambda h, i: (0, 0)),
                  pl.BlockSpec((1, hw), lambda h, i: (0, 0)),
                  pl.BlockSpec((bq, hw), lambda h, i: (i, h)),
                  pl.BlockSpec((m, hw), lambda h, i: (0, h)),
                  pl.BlockSpec((m, hw), lambda h, i: (0, h))],
        out_specs=pl.BlockSpec((bq, hw), lambda h, i: (i, h)),
        out_shape=jax.ShapeDtypeStruct((m, width), BF16),
        scratch_shapes=scratch,
        compiler_params=_cparams(("parallel", "arbitrary"), vmem),
        name="prompt_attn",
    )(lam_rows, subln_w.reshape(1, hw), q, kb, vb)


def _split3_bf16(x):
    hi = x.astype(BF16)
    r1 = x - hi.astype(F32)
    mid = r1.astype(BF16)
    lo = (r1 - mid.astype(F32)).astype(BF16)
    return hi, mid, lo


def _group_row(x, group, row):
    n, lanes = x.shape
    x3 = x.reshape(n // group, group, lanes)
    return jnp.broadcast_to(x3[:, row:row + 1, :], x3.shape).reshape(n, lanes)


def _hgrn_chunk(qr, g, kk, v, st, tril, dsum):
    c, kd = g.shape
    vb = v.astype(BF16)
    g3 = jnp.concatenate(_split3_bf16(g), axis=1)
    b3 = _dot(tril, g3)
    b = b3[:, :kd] + b3[:, kd:2 * kd] + b3[:, 2 * kd:]
    b_last = b[c - 1:c, :]
    row = lax.broadcasted_iota(jnp.int32, (c, kd), 0)
    trow = lax.broadcasted_iota(jnp.int32, (c, c), 0)
    tcol = lax.broadcasted_iota(jnp.int32, (c, c), 1)

    o = _dot_nt((qr * jnp.exp2(b)).astype(BF16), st.astype(BF16))
    k_state = (kk * jnp.exp2(b_last - b)).astype(BF16)
    st_new = jnp.exp2(b_last) * st + _dot_tn(vb, k_state)

    b_k = b - jnp.log2(kk)
    parts = [(qr * jnp.exp2(jnp.minimum(b - _group_row(b_k, SUBLANES, j), 0.0))).astype(BF16)
             for j in range(SUBLANES)]
    in_group = (trow // SUBLANES == tcol // SUBLANES) & (tcol % SUBLANES <= trow % SUBLANES)
    a = jnp.where(in_group, _dot(jnp.concatenate(parts, axis=1), dsum), 0.0)

    m = SUBLANES
    while m < c:
        e = jnp.exp2(-jnp.abs(b - _group_row(b, 2 * m, m - 1)))
        right = (row % (2 * m)) >= m
        qm = jnp.where(right, qr * e, 0.0).astype(BF16)
        km = jnp.where(right, 0.0, kk * e).astype(BF16)
        a = a + jnp.where(trow // (2 * m) == tcol // (2 * m), _dot_nt(qm, km), 0.0)
        m *= 2

    o = o + _dot(a.astype(BF16), vb)
    return o, st_new


def _prompt_hgrn_kernel(q_ref, g_ref, k_ref, v_ref, t_ref, nw_ref, tril_ref, dsum_ref, o_ref, s_ref, st_ref, *,
                        chunk):
    ci = pl.program_id(1)

    @pl.when(ci == 0)
    def _():
        st_ref[...] = jnp.zeros(st_ref.shape, F32)

    tril = tril_ref[...]
    dsum = dsum_ref[...]
    hb, bt, kd = q_ref.shape
    st = [st_ref[h] for h in range(hb)]
    for sub in range(bt // chunk):
        rows = slice(sub * chunk, (sub + 1) * chunk)
        for h in range(hb):
            o, st[h] = _hgrn_chunk(q_ref[h, rows, :], g_ref[h, rows, :], k_ref[h, rows, :], v_ref[h, rows, :],
                                   st[h], tril, dsum)
            ms = jnp.mean(o * o, axis=-1, keepdims=True)
            o_ref[rows, h * kd:(h + 1) * kd] = (o * lax.rsqrt(ms + EPS) * nw_ref[...]
                                                * t_ref[h, rows, :]).astype(o_ref.dtype)
    for h in range(hb):
        st_ref[h] = st[h]

    @pl.when(ci == pl.num_programs(1) - 1)
    def _():
        for h in range(hb):
            s_ref[h] = st[h].T


def _prompt_hgrn(qr, g, kk, vr, gate, rec_norm_w, *, chunk, bt, heads_per_step):
    nh, m, kd = qr.shape
    hb = heads_per_step
    assert vr.shape[2] == kd and nh % hb == 0
    tril = jnp.tril(jnp.ones((chunk, chunk), F32)).astype(BF16)
    dsum = (jnp.arange(SUBLANES * kd)[:, None] // kd == jnp.arange(chunk)[None, :] % SUBLANES).astype(BF16)
    ispec = pl.BlockSpec((hb, bt, kd), lambda h, c: (h, c, 0))
    vmem = hb * (2 * (5 * bt * kd * 4 + bt * kd * 2) + 64 * chunk * kd * 4) + (4 << 20)
    return pl.pallas_call(
        functools.partial(_prompt_hgrn_kernel, chunk=chunk),
        grid=(nh // hb, m // bt),
        in_specs=[ispec] * 5 + [pl.BlockSpec((1, kd), lambda h, c: (0, 0)),
                                pl.BlockSpec((chunk, chunk), lambda h, c: (0, 0)),
                                pl.BlockSpec((SUBLANES * kd, chunk), lambda h, c: (0, 0))],
        out_specs=[pl.BlockSpec((bt, hb * kd), lambda h, c: (c, h)),
                   pl.BlockSpec((hb, kd, kd), lambda h, c: (h, 0, 0))],
        out_shape=[jax.ShapeDtypeStruct((m, nh * kd), BF16),
                   jax.ShapeDtypeStruct((nh, kd, kd), F32)],
        scratch_shapes=[pltpu.VMEM((hb, kd, kd), F32)],
        compiler_params=_cparams(("parallel", "arbitrary"), vmem),
        name="prompt_hgrn",
    )(qr, g, kk, vr, gate, rec_norm_w.reshape(1, kd), tril, dsum)


def _swap_halves(x, dh):
    return jnp.concatenate([x[..., dh:], x[..., :dh]], axis=-1)


def _sample_attn_kernel(pt_ref, lam_ref, sub_ref, spread_ref, q_ref, ks_ref, vs_ref, pool_k_ref, pool_v_ref, o_ref,
                        st_ref, m_ref, l_ref, same_ref, cross_ref, buf_ref, sem_ref, *, dh, lam_init):
    n_slots, g, page, n_heads, hw = buf_ref.shape
    step = pl.program_id(1)
    steps_per_seq = pl.num_programs(1)
    n_steps = steps_per_seq // 2
    cols = page * n_heads

    t = pl.program_id(0) * steps_per_seq + step
    total = pl.num_programs(0) * steps_per_seq
    ahead = n_slots - 1

    def start_step(tt):
        seq = tt // steps_per_seq
        s = tt % steps_per_seq
        slot = tt % n_slots
        first = jnp.where(s < n_steps, s, s - n_steps) * g

        def issue(pool_ref):
            for i in range(g):
                pltpu.make_async_copy(pool_ref.at[pt_ref[seq, first + i]], buf_ref.at[slot, i],
                                      sem_ref.at[slot]).start()

        @pl.when(s < n_steps)
        def _():
            issue(pool_k_ref)

        @pl.when(s >= n_steps)
        def _():
            issue(pool_v_ref)

    @pl.when(t == 0)
    def _():
        for tt in range(ahead):
            start_step(jnp.int32(tt))

    @pl.when(t + ahead < total)
    def _():
        start_step(t + ahead)

    slot = t % n_slots
    for i in range(g):
        pltpu.make_async_copy(pool_k_ref.at[0], buf_ref.at[slot, i], sem_ref.at[slot]).wait()
    q = q_ref[...].astype(F32)
    lane = lax.broadcasted_iota(jnp.int32, (n_heads, hw), 1)
    q_rows = jnp.concatenate([jnp.where(lane < dh, q, 0.0), jnp.where(lane < dh, 0.0, q)], axis=0).astype(BF16)
    own_head = (lax.broadcasted_iota(jnp.int32, (2 * n_heads, cols), 1) % n_heads
                == lax.broadcasted_iota(jnp.int32, (2 * n_heads, cols), 0) % n_heads)
    x_self = ks_ref[...] * q
    s_self = [jnp.sum(x_self[:, c * dh:(c + 1) * dh], axis=1, keepdims=True) for c in range(2)]

    def spread(col):
        sel = (lax.broadcasted_iota(jnp.int32, (2 * n_heads, LANES), 1)
               == lax.broadcasted_iota(jnp.int32, (2 * n_heads, LANES), 0) % n_heads)
        terms = _split3_bf16(jnp.where(sel, col, 0.0))
        wide = _dot_tn(jnp.concatenate(terms, axis=0), jnp.concatenate([spread_ref[...]] * 3, axis=0))
        return wide[:n_heads, :]

    @pl.when(step == 0)
    def _():
        m_ref[...] = jnp.full(m_ref.shape, NEG_INF, F32)
        l_ref[...] = jnp.zeros(l_ref.shape, F32)
        same_ref[...] = jnp.zeros(same_ref.shape, F32)
        cross_ref[...] = jnp.zeros(cross_ref.shape, F32)

    @pl.when(step < n_steps)
    def _():
        m = m_ref[...]
        for i in range(g):
            keys = buf_ref[slot, i].reshape(cols, hw).astype(BF16)
            s = jnp.where(own_head, _dot_nt(q_rows, keys), NEG_INF)
            st_ref[step * g + i] = s
            m = jnp.maximum(m, jnp.max(s, axis=1, keepdims=True))
        m_ref[...] = m

    @pl.when(step == n_steps - 1)
    def _():
        m_ref[...] = jnp.maximum(m_ref[...], jnp.concatenate(s_self, axis=0))

    @pl.when(step >= n_steps)
    def _():
        m = m_ref[...]
        l = l_ref[...]
        same = same_ref[...]
        cross = cross_ref[...]
        for i in range(g):
            p = jnp.exp(st_ref[(step - n_steps) * g + i] - m)
            l = l + jnp.sum(p, axis=1, keepdims=True)
            p_page = _dot_tn(p.astype(BF16), spread_ref[...]).reshape(page, n_heads, hw)
            v = buf_ref[slot, i]
            same = same + jnp.sum(p_page * v, axis=0)
            cross = cross + jnp.sum(p_page * _swap_halves(v, dh), axis=0)
        l_ref[...] = l
        same_ref[...] = same
        cross_ref[...] = cross

    @pl.when(step == 2 * n_steps - 1)
    def _():
        s_wide = jnp.concatenate([jnp.broadcast_to(s, (n_heads, dh)) for s in s_self], axis=1)
        p_self = jnp.exp(s_wide - spread(m_ref[...]))
        v_self = vs_ref[...]
        l = spread(l_ref[...]) + p_self
        same = same_ref[...] + p_self * v_self
        cross = cross_ref[...] + p_self * _swap_halves(v_self, dh)
        acc0 = jnp.concatenate([same[:, :dh], cross[:, :dh]], axis=-1)
        acc1 = jnp.concatenate([cross[:, dh:], same[:, dh:]], axis=-1)
        l0 = jnp.concatenate([l[:, :dh], l[:, :dh]], axis=-1)
        l1 = jnp.concatenate([l[:, dh:], l[:, dh:]], axis=-1)
        lam = _lambda_full(lam_ref, lam_init)
        o = _diff_combine(acc0, l0, acc1, l1, lam, sub_ref[...], 1.0 - lam_init)
        o_ref[...] = o.astype(o_ref.dtype)


def _sample_attn(q, k_self, v_self, pool_k, pool_v, page_ids, lam_rows, subln_w, *, n_heads, dh, lam_init,
                 pages_per_step, ring_slots):
    nb, width = q.shape
    n_pages = page_ids.shape[1]
    page = pool_k.shape[1]
    hw = 2 * dh
    g = pages_per_step
    assert n_pages % g == 0 and pool_k.shape[2:] == (n_heads, hw) and width == n_heads * hw
    steps = n_pages // g
    assert 2 <= ring_slots <= nb * 2 * steps
    spread = (jnp.arange(2 * n_heads)[:, None] // n_heads == jnp.arange(hw)[None, :] // dh).astype(BF16)

    row_spec = pl.BlockSpec((None, n_heads, hw), lambda b, p, pt: (b, 0, 0))
    pool_spec = pl.BlockSpec(memory_space=pl.ANY)
    in_specs = [pl.BlockSpec((4, dh), lambda b, p, pt: (0, 0)),
                pl.BlockSpec((1, hw), lambda b, p, pt: (0, 0)),
                pl.BlockSpec((2 * n_heads, hw), lambda b, p, pt: (0, 0)),
                row_spec, row_spec, row_spec, pool_spec, pool_spec]
    page_bytes = page * width * 4
    score_bytes = n_pages * 2 * n_heads * page * n_heads * 4
    vmem = ring_slots * g * page_bytes + score_bytes + 6 * page_bytes + (4 << 20)
    scratch = [pltpu.VMEM((n_pages, 2 * n_heads, page * n_heads), F32),
               pltpu.VMEM((2 * n_heads, 1), F32), pltpu.VMEM((2 * n_heads, 1), F32),
               pltpu.VMEM((n_heads, hw), F32), pltpu.VMEM((n_heads, hw), F32),
               pltpu.VMEM((ring_slots, g, page, n_heads, hw), F32),
               pltpu.SemaphoreType.DMA((ring_slots,))]
    out = pl.pallas_call(
        functools.partial(_sample_attn_kernel, dh=dh, lam_init=lam_init),
        grid_spec=pltpu.PrefetchScalarGridSpec(
            num_scalar_prefetch=1,
            grid=(nb, 2 * steps),
            in_specs=in_specs,
            out_specs=row_spec,
            scratch_shapes=scratch),
        out_shape=jax.ShapeDtypeStruct((nb, n_heads, hw), BF16),
        compiler_params=_cparams(("arbitrary", "arbitrary"), vmem),
        name="sample_attn",
    )(page_ids, lam_rows, subln_w.reshape(1, hw), spread, q.reshape(nb, n_heads, hw),
      k_self.reshape(nb, n_heads, hw), v_self.reshape(nb, n_heads, hw), pool_k, pool_v)
    return out.reshape(nb, width)


def _hi_lo(x):
    hi = x.astype(BF16).astype(F32)
    return hi, x - hi


def _sample_hgrn_kernel(q_ref, g_ref, k_ref, v_ref, t_ref, nw_ref, s_ref, o_ref, so_ref, orow_ref):
    nh, nb, kd = q_ref.shape
    f_t = jnp.exp2(g_ref[...]).reshape(nh * nb, kd).T
    own = (lax.broadcasted_iota(jnp.int32, (nb, nb * kd), 1) // kd
           == lax.broadcasted_iota(jnp.int32, (nb, nb * kd), 0))

    def own_block(x):
        return jnp.where(own, jnp.concatenate([x] * nb, axis=1), 0.0)

    for h in range(nh):
        k_hi, k_lo = _hi_lo(k_ref[h])
        v_hi, v_lo = _hi_lo(v_ref[h])
        ka = jnp.concatenate([k_hi, k_hi, k_lo, jnp.zeros_like(k_lo)], axis=0).astype(BF16)
        vb = jnp.concatenate([own_block(v_hi), own_block(v_lo), own_block(v_hi),
                              jnp.zeros((nb, nb * kd), F32)], axis=0).astype(BF16)
        kv = _dot_tn(ka, vb)
        s_new = []
        for i in range(nb):
            r = h * nb + i
            s_i = f_t[:, r:r + 1] * s_ref[i, h] + kv[:, i * kd:(i + 1) * kd]
            so_ref[i, h] = s_i
            s_new.append(s_i.astype(BF16))
        q_rows = jnp.concatenate([q_ref[h], jnp.zeros((nb, kd), F32)], axis=0).astype(BF16)
        o_all = _dot(q_rows, jnp.concatenate(s_new, axis=1))
        o = jnp.concatenate([o_all[i:i + 1, i * kd:(i + 1) * kd] for i in range(nb)], axis=0)
        ms = jnp.mean(o * o, axis=-1, keepdims=True)
        orow_ref[:, h * kd:(h + 1) * kd] = o * lax.rsqrt(ms + EPS) * nw_ref[...] * t_ref[h]
    o_ref[...] = orow_ref[...].astype(o_ref.dtype)


def _sample_hgrn(qr, g, kk, vr, gate, rec_norm_w, state, *, bb):
    nh, nb, kd = qr.shape
    vd = vr.shape[2]
    assert state.shape == (nb, nh, kd, vd) and kd == vd
    ispec = pl.BlockSpec((nh, bb, kd), lambda b: (0, b, 0))
    sspec = pl.BlockSpec((bb, nh, kd, vd), lambda b: (b, 0, 0, 0))
    vmem = 2 * 2 * bb * nh * kd * vd * 4 + 2 * 5 * nh * bb * kd * 4 + 8 * nh * bb * kd * 4 + (4 << 20)
    return pl.pallas_call(
        _sample_hgrn_kernel,
        grid=(nb // bb,),
        in_specs=[ispec] * 5 + [pl.BlockSpec((1, vd), lambda b: (0, 0)), sspec],
        out_specs=[pl.BlockSpec((bb, nh * vd), lambda b: (b, 0)), sspec],
        out_shape=[jax.ShapeDtypeStruct((nb, nh * vd), BF16),
                   jax.ShapeDtypeStruct(state.shape, F32)],
        scratch_shapes=[pltpu.VMEM((bb, nh * vd), F32)],
        compiler_params=_cparams(("parallel",), vmem),
        name="sample_hgrn",
    )(qr, g, kk, vr, gate, rec_norm_w.reshape(1, vd), state)


PROJ_ROWS = 1024
PROJ_COLS = 256
PROMPT_Q_BLOCK = 512
HGRN_CHUNK = 128
HGRN_STEP = 512
HGRN_HEADS = 8
FFN_ROWS = 512
OUT_ROWS = 1024
FFN_COLS = 512
SAMPLE_FFN_COLS = 256
OUT_COLS = 1024
SAMPLE_PAGES_PER_STEP = 8
SAMPLE_RING_SLOTS = 4
SAMPLE_HGRN_SEQS = 8


def _mixer_inputs(x, l, dims, norm1_w, att_weights, rec_weights, lb_logits, *, q_scale, emit_weights):
    m = x.shape[0]
    xn = _rmsnorm(x, norm1_w[l], bm=min(m, 256))
    att, att_b = _proj_att(xn, att_weights, col0=0, att_width=dims["att_width"], scale=q_scale,
                           emit_weights=emit_weights, bm=min(m, PROJ_ROWS), bn=PROJ_COLS)
    rec, rec_b = _proj_rec(xn, rec_weights, lb_logits, layer=l, col0=3 * dims["att_width"],
                           rec_width=dims["rec_width"], hd=dims["kd"], emit_weights=emit_weights,
                           bm=min(m, PROJ_ROWS), bn=PROJ_COLS)
    return att, rec, att_b, rec_b


def kernel(x_prompt, x_sample, cache_k, cache_v, state_hgrn, page_table, norm1_w, w_in, lambda_q1, lambda_k1,
           lambda_q2, lambda_k2, subln_w, lb_logits, rec_norm_w, w_out, norm2_w, w_up, w_down, final_norm_w):
    batch, seq, d_model = x_prompt.shape
    dec_batch, dec_seq, _ = x_sample.shape
    depth, n_phys, page, n_att_heads, hw = cache_k.shape
    _, _, n_rec_heads, kd, vd = state_hgrn.shape
    assert batch == 1 and dec_seq == 1 and kd == vd
    dh = hw // 2
    dims = dict(att_width=n_att_heads * hw, rec_width=n_rec_heads * kd, dh=dh, kd=kd)
    aw = dims["att_width"]

    w_out_b = w_out.astype(BF16)
    pool_k = cache_k.reshape(depth * n_phys, page, n_att_heads, hw)
    pool_v = cache_v.reshape(depth * n_phys, page, n_att_heads, hw)

    hp = x_prompt.reshape(seq, d_model)
    hs = x_sample.reshape(dec_batch, d_model)
    outs = [[] for _ in range(6)]
    for l in range(depth):
        lam_init = 0.8 - 0.6 * math.exp(-0.3 * l)
        lam_rows = jnp.stack([lambda_q1[l], lambda_k1[l], lambda_q2[l], lambda_k2[l]])
        last = l == depth - 1

        att_scale = dh ** -0.5
        (q, k, _, v, _), (qr, g, kk, vr, gate), w_att_b, w_rec_b = _mixer_inputs(
            hs, l, dims, norm1_w, w_in[l], w_in[l], lb_logits, q_scale=att_scale, emit_weights=True)
        o_att = _sample_attn(q, k, v, pool_k, pool_v, page_table + l * n_phys, lam_rows, subln_w[l],
                             n_heads=n_att_heads, dh=dh, lam_init=lam_init, pages_per_step=SAMPLE_PAGES_PER_STEP,
                             ring_slots=SAMPLE_RING_SLOTS)
        o_rec, s_s = _sample_hgrn(qr, g, kk, vr, gate, rec_norm_w[l], state_hgrn[l], bb=SAMPLE_HGRN_SEQS)
        h = _out_proj(hs, o_att, o_rec, w_out_b[l], bm=dec_batch, bn=OUT_COLS)
        hs, w_up_b, w_down_b = _ffn(h, norm2_w[l], w_up[l], w_down[l], final_norm_w, final_norm=last,
                                    emit_weights=True, bm=dec_batch, bf=SAMPLE_FFN_COLS)
        outs[3].append(k.reshape(dec_batch, dec_seq, n_att_heads, hw))
        outs[4].append(v.reshape(dec_batch, dec_seq, n_att_heads, hw))
        outs[5].append(s_s)

        (q, k, kb, v, vb), (qr, g, kk, vr, gate), _, _ = _mixer_inputs(
            hp, l, dims, norm1_w, w_att_b, w_rec_b, lb_logits, q_scale=att_scale * math.log2(math.e),
            emit_weights=False)
        o_att = _prompt_attn(q, kb, vb, lam_rows, subln_w[l], n_heads=n_att_heads, dh=dh, lam_init=lam_init,
                             bq=PROMPT_Q_BLOCK)
        o_rec, s_p = _prompt_hgrn(qr, g, kk, vr, gate, rec_norm_w[l], chunk=HGRN_CHUNK, bt=HGRN_STEP,
                                  heads_per_step=HGRN_HEADS)
        h = _out_proj(hp, o_att, o_rec, w_out_b[l], bm=OUT_ROWS, bn=OUT_COLS)
        hp = _ffn(h, norm2_w[l], w_up_b, w_down_b, final_norm_w, final_norm=last, emit_weights=False,
                  bm=FFN_ROWS, bf=FFN_COLS)
        outs[0].append(k.reshape(batch, seq, n_att_heads, hw))
        outs[1].append(v.reshape(batch, seq, n_att_heads, hw))
        outs[2].append(s_p.reshape(batch, n_rec_heads, kd, vd))

    return (hp.reshape(batch, seq, d_model), hs.reshape(dec_batch, dec_seq, d_model),
            jnp.stack(outs[0]), jnp.stack(outs[1]), jnp.stack(outs[2]),
            jnp.stack(outs[3]), jnp.stack(outs[4]), jnp.stack(outs[5]))
```
